```python
import math, functools
import jax, jax.numpy as jnp
from jax import lax
import numpy as np

D_MODEL = 2048
BATCH = 32
SEQ = 256
DEPTH = 4
DEC_BATCH = 8
DEC_SEQ = 2048
PAST_LEN = 512

GRID_W = 64
BRANCH_W = 512
N_BRANCH = 4
SSM_GROUP = 16
SSM_GROUPS = BRANCH_W // SSM_GROUP
SSM_STATE = 64
N_DIR = 2
FFT_GROUPS = 4
FFT_GROUP_W = BRANCH_W // FFT_GROUPS
CHUNK = 128
SGU_HEADS = 4
SGU_HEAD_W = BRANCH_W // SGU_HEADS
CONV_W = 3
IN_COLS = 7 * BRANCH_W
D_FF = 5504
N_EXPERTS = 8
TOP_K = 2
D_FF_EXPERT = 7168
N_DENSE = (DEPTH + 1) // 2
N_MOE = DEPTH // 2
N_MOD = 6
EPS = 1e-6

kernel_name = "hybrid_s5_fnet_gmlp_conv_diffusion_step"


def rmsnorm(x, g):
    xf = x.astype(jnp.float32)
    y = xf * lax.rsqrt(jnp.mean(xf * xf, axis=-1, keepdims=True) + EPS)
    return (y * g.astype(jnp.float32)).astype(x.dtype)


def _lin_combine(left, right):
    a_l, b_l = left
    a_r, b_r = right
    return a_r * a_l, a_r * b_l + b_r


def s5_scan(bu, lam_bar, h0, reverse):
    if h0 is not None:
        idx = -1 if reverse else 0
        bu = bu.at[:, idx].add(lam_bar * h0)
    a = jnp.broadcast_to(lam_bar, bu.shape)
    _, h = lax.associative_scan(_lin_combine, (a, bu), reverse=reverse, axis=1)
    return h


def s5_branch(u, lp, init_re, init_im):
    B_, L, _ = u.shape
    f32 = jnp.float32
    ug = u.astype(f32).reshape(B_, L, SSM_GROUPS, SSM_GROUP).astype(jnp.complex64)
    lam = lax.complex(lp["lam_re"].astype(f32), lp["lam_im"].astype(f32))
    dt = jnp.exp(lp["log_dt"].astype(f32))[..., None]
    lam_bar = jnp.exp(lam * dt)
    bmat = lax.complex(lp["b_re"].astype(f32), lp["b_im"].astype(f32))
    b_bar = ((lam_bar - 1.0) / lam)[..., None] * bmat
    cmat = lax.complex(lp["c_re"].astype(f32), lp["c_im"].astype(f32))
    y = u.astype(f32) * lp["d"].astype(f32)
    finals = []
    for dr in range(N_DIR):
        reverse = dr == 1
        bu = jnp.einsum('gph,blgh->blgp', b_bar[dr], ug)
        h0 = None if init_re is None else lax.complex(init_re[:, dr].astype(f32), init_im[:, dr].astype(f32))
        h = s5_scan(bu, lam_bar[dr], h0, reverse)
        y = y + jnp.real(jnp.einsum('ghp,blgp->blgh', cmat[dr], h)).reshape(B_, L, BRANCH_W)
        if init_re is None:
            finals.append(h[:, 0] if reverse else h[:, -1])
    z = jax.nn.gelu(y.astype(u.dtype))
    out = z * jax.nn.sigmoid(z @ lp["w_glu"])
    if init_re is None:
        fin = jnp.stack(finals, axis=1)
        return out, jnp.real(fin), jnp.imag(fin)
    return out, None, None


def fourier_branch(u):
    B_, L, _ = u.shape
    ug = u.astype(jnp.float32).reshape(B_, L, FFT_GROUPS, FFT_GROUP_W)
    f = jnp.fft.fft2(ug, axes=(1, 3), norm="ortho")
    return jnp.real(f).reshape(B_, L, BRANCH_W).astype(u.dtype)


def sgu_branch(uv, g_norm, w_s, b_s):
    u, v = jnp.split(jax.nn.gelu(uv), 2, axis=-1)
    v = rmsnorm(v, g_norm)
    B_, L, _ = v.shape
    vc = v.reshape(B_, L // CHUNK, CHUNK, SGU_HEADS, SGU_HEAD_W)
    s = jnp.einsum('hst,bnthc->bnshc', w_s, vc) + b_s.T[None, None, :, :, None]
    return u * s.reshape(B_, L, BRANCH_W)


def conv3(z, w):
    L = z.shape[-2]
    zp = jnp.pad(z, [(0, 0)] * (z.ndim - 2) + [(1, 1), (0, 0)])
    return zp[..., :L, :] * w[:, 0] + zp[..., 1:L + 1, :] * w[:, 1] + zp[..., 2:, :] * w[:, 2]


def short_conv_branch(z, w, latent):
    xin, bg, cg = jnp.split(z, 3, axis=-1)
    t = cg * xin
    if latent:
        B_, L, C = t.shape
        rows = L // GRID_W
        t = conv3(t.reshape(B_, rows, GRID_W, C), w).reshape(B_, L, C)
    else:
        t = conv3(t, w)
    return bg * t


def swiglu(h, w1, w3, w2):
    return (jax.nn.silu(h @ w1) * (h @ w3)) @ w2


def moe_swiglu(h, w_r, b_r, w1, w3, w2):
    B_, L, D = h.shape
    t = h.reshape(-1, D)
    logits = (t @ w_r).astype(jnp.float32) + b_r.astype(jnp.float32)
    top_v, top_i = lax.top_k(logits, TOP_K)
    probs = jax.nn.softmax(top_v, axis=-1)
    combine = jnp.sum(jax.nn.one_hot(top_i, N_EXPERTS, dtype=jnp.float32) * probs[..., None], axis=1).astype(t.dtype)
    out = jnp.zeros_like(t)
    for e in range(N_EXPERTS):
        out = out + combine[:, e:e + 1] * swiglu(t, w1[e], w3[e], w2[e])
    return out.reshape(B_, L, D)


def trunk_layer(x, mod, lp, ffn, latent, init_re, init_im):
    B_, L, _ = x.shape
    shift1, scale1, gate1, shift2, scale2, gate2 = [mod[:, k][:, None, :] for k in range(N_MOD)]
    h = rmsnorm(x, lp["norm1"]) * (1 + scale1) + shift1
    z = h @ lp["w_in"]
    z_ssm, z_fft, z_sgu, z_conv = jnp.split(z, [BRANCH_W, 2 * BRANCH_W, 4 * BRANCH_W], axis=-1)
    y_ssm, fin_re, fin_im = s5_branch(z_ssm, lp, init_re, init_im)
    y_fft = fourier_branch(z_fft)
    y_sgu = sgu_branch(z_sgu, lp["sgu_norm"], lp["w_s"], lp["b_s"])
    y_conv = short_conv_branch(z_conv, lp["conv_w"], latent)
    br = jnp.stack([y_ssm, y_fft, y_sgu, y_conv], axis=2)
    up = jnp.einsum('blnc,ncd->blnd', br, lp["w_up"])
    gates = jax.nn.sigmoid((h @ lp["w_gate"]).reshape(B_, L, N_BRANCH, D_MODEL))
    merged = jnp.einsum('blnd,blnd->bld', gates, up)
    x = x + gate1 * (merged @ lp["w_o"])
    h2 = rmsnorm(x, lp["norm2"]) * (1 + scale2) + shift2
    x = x + gate2 * ffn(h2)
    return x, fin_re, fin_im


def setup_inputs(seed: int = 0) -> dict:
    key = jax.random.key(seed)
    ks = iter(jax.random.split(key, 48))
    f32 = jnp.float32
    D = D_MODEL

    def nrm(shape, s):
        return s * jax.random.normal(next(ks), shape, f32)

    sshape = (DEPTH, N_DIR, SSM_GROUPS, SSM_STATE)
    return {
        "x_prompt": nrm((BATCH, SEQ, D), 1.0),
        "x_sample": nrm((DEC_BATCH, DEC_SEQ, D), 1.0),
        "c": nrm((DEC_BATCH, D), 1.0),
        "state_ssm_re": nrm((DEC_BATCH, DEPTH, N_DIR, SSM_GROUPS, SSM_STATE), 0.5),
        "state_ssm_im": nrm((DEC_BATCH, DEPTH, N_DIR, SSM_GROUPS, SSM_STATE), 0.5),
        "c_ctx": nrm((D,), 1.0),
        "norm1_g": 1.0 + nrm((DEPTH, D), 0.02),
        "norm2_g": 1.0 + nrm((DEPTH, D), 0.02),
        "w_ada": nrm((DEPTH, D, N_MOD * D), 0.5 * D ** -0.5),
        "b_ada": nrm((DEPTH, N_MOD * D), 0.02),
        "w_in": nrm((DEPTH, D, IN_COLS), D ** -0.5),
        "w_gate": nrm((DEPTH, D, N_BRANCH * D), D ** -0.5),
        "ssm_lam_re": -0.5 + nrm(sshape, 0.01),
        "ssm_lam_im": math.pi * jnp.arange(SSM_STATE, dtype=f32) + nrm(sshape, 0.01),
        "ssm_b_re": nrm((DEPTH, N_DIR, SSM_GROUPS, SSM_STATE, SSM_GROUP), (2 * SSM_GROUP) ** -0.5),
        "ssm_b_im": nrm((DEPTH, N_DIR, SSM_GROUPS, SSM_STATE, SSM_GROUP), (2 * SSM_GROUP) ** -0.5),
        "ssm_c_re": nrm((DEPTH, N_DIR, SSM_GROUPS, SSM_GROUP, SSM_STATE), (2 * SSM_STATE) ** -0.5),
        "ssm_c_im": nrm((DEPTH, N_DIR, SSM_GROUPS, SSM_GROUP, SSM_STATE), (2 * SSM_STATE) ** -0.5),
        "ssm_log_dt": jax.random.uniform(next(ks), (DEPTH, N_DIR, SSM_GROUPS), f32, math.log(1e-3), math.log(1e-1)),
        "ssm_d": nrm((DEPTH, BRANCH_W), 1.0),
        "ssm_w_glu": nrm((DEPTH, BRANCH_W, BRANCH_W), BRANCH_W ** -0.5),
        "sgu_norm_g": 1.0 + nrm((DEPTH, BRANCH_W), 0.02),
        "sgu_w_spatial": nrm((DEPTH, SGU_HEADS, CHUNK, CHUNK), CHUNK ** -0.5),
        "sgu_b_spatial": 1.0 + nrm((DEPTH, SGU_HEADS, CHUNK), 0.02),
        "conv_w": nrm((DEPTH, BRANCH_W, CONV_W), CONV_W ** -0.5),
        "w_up": nrm((DEPTH, N_BRANCH, BRANCH_W, D), BRANCH_W ** -0.5),
        "w_o": nrm((DEPTH, D, D), D ** -0.5),
        "ffn_w1": nrm((N_DENSE, D, D_FF), D ** -0.5),
        "ffn_w3": nrm((N_DENSE, D, D_FF), D ** -0.5),
        "ffn_w2": nrm((N_DENSE, D_FF, D), D_FF ** -0.5),
        "moe_w_router": nrm((N_MOE, D, N_EXPERTS), D ** -0.5),
        "moe_b_router": nrm((N_MOE, N_EXPERTS), 0.01),
        "moe_w1": nrm((N_MOE, N_EXPERTS, D, D_FF_EXPERT), D ** -0.5),
        "moe_w3": nrm((N_MOE, N_EXPERTS, D, D_FF_EXPERT), D ** -0.5),
        "moe_w2": nrm((N_MOE, N_EXPERTS, D_FF_EXPERT, D), D_FF_EXPERT ** -0.5),
        "final_norm_g": 1.0 + nrm((D,), 0.02),
    }


def reference(x_prompt, x_sample, c, state_ssm_re, state_ssm_im, c_ctx,
              norm1_g, norm2_g, w_ada, b_ada, w_in, w_gate,
              ssm_lam_re, ssm_lam_im, ssm_b_re, ssm_b_im, ssm_c_re, ssm_c_im, ssm_log_dt, ssm_d, ssm_w_glu,
              sgu_norm_g, sgu_w_spatial, sgu_b_spatial, conv_w, w_up, w_o,
              ffn_w1, ffn_w3, ffn_w2, moe_w_router, moe_b_router, moe_w1, moe_w3, moe_w2, final_norm_g):
    cond_ctx = jax.nn.silu(c_ctx)[None, :]
    cond_lat = jax.nn.silu(c)
    xp, xs = x_prompt, x_sample
    new_re, new_im = [], []
    for i in range(DEPTH):
        lp = {
            "norm1": norm1_g[i], "norm2": norm2_g[i], "w_in": w_in[i], "w_gate": w_gate[i],
            "lam_re": ssm_lam_re[i], "lam_im": ssm_lam_im[i], "b_re": ssm_b_re[i], "b_im": ssm_b_im[i],
            "c_re": ssm_c_re[i], "c_im": ssm_c_im[i], "log_dt": ssm_log_dt[i], "d": ssm_d[i],
            "w_glu": ssm_w_glu[i], "sgu_norm": sgu_norm_g[i], "w_s": sgu_w_spatial[i],
            "b_s": sgu_b_spatial[i], "conv_w": conv_w[i], "w_up": w_up[i], "w_o": w_o[i],
        }
        j = i // 2
        if i % 2 == 0:
            ffn = functools.partial(swiglu, w1=ffn_w1[j], w3=ffn_w3[j], w2=ffn_w2[j])
        else:
            ffn = functools.partial(moe_swiglu, w_r=moe_w_router[j], b_r=moe_b_router[j],
                                    w1=moe_w1[j], w3=moe_w3[j], w2=moe_w2[j])
        mod_ctx = (cond_ctx @ w_ada[i] + b_ada[i]).reshape(1, N_MOD, D_MODEL)
        mod_lat = (cond_lat @ w_ada[i] + b_ada[i]).reshape(-1, N_MOD, D_MODEL)
        xp, fin_re, fin_im = trunk_layer(xp, mod_ctx, lp, ffn, False, None, None)
        xs, _, _ = trunk_layer(xs, mod_lat, lp, ffn, True, state_ssm_re[:, i], state_ssm_im[:, i])
        new_re.append(fin_re)
        new_im.append(fin_im)
    y_prompt = rmsnorm(xp, final_norm_g)
    y_sample = rmsnorm(xs, final_norm_g)
    new_ssm_re = jnp.stack(new_re, axis=1)
    new_ssm_im = jnp.stack(new_im, axis=1)
    return (y_prompt, y_sample, new_ssm_re, new_ssm_im)
```

```python
import functools
import math

import jax
import jax.numpy as jnp
from jax import lax
from jax.experimental import pallas as pl
from jax.experimental.pallas import tpu as pltpu

F32 = jnp.float32
BF16 = jnp.bfloat16
MXU_DTYPE = BF16

D_MODEL = 2048
BRANCH_W = 512
N_BRANCH = 4
SSM_GROUP = 16
SSM_GROUPS = 32
SSM_STATE = 64
CHUNK = 128
SSM_CENTER = CHUNK // 2
SGU_HEADS = 4
GRID_W = 64
D_FF = 5504
D_FF_PAD = 5632
N_EXPERTS = 8
N_MOD = 6
EPS = 1e-6
COND_ROWS = 16

TM = 512
TF = 512
MOE_TM = 512
GATHER_ROWS = 256
VMEM_CAP = 60 * 1024 * 1024


def _vmem_limit(pipelined_bytes, scratch_bytes=0):
    est = 2 * pipelined_bytes + scratch_bytes + (12 << 20)
    return int(min(VMEM_CAP, max(est, 32 << 20)))


def _params(sem, pipelined_bytes, scratch_bytes=0):
    return pltpu.CompilerParams(dimension_semantics=sem,
                                vmem_limit_bytes=_vmem_limit(pipelined_bytes, scratch_bytes))


def _nbytes(shape, dtype):
    return math.prod(shape) * jnp.dtype(dtype).itemsize


def _seq_row(i, tm, n_ctx, lat_len):
    start = i * tm
    return jnp.where(start < n_ctx, 0, 1 + (start - n_ctx) // lat_len)


def _ada_kernel(c_ref, w_ref, b_ref, o_ref):
    c = c_ref[...]
    s = (c * jax.nn.sigmoid(c)).astype(MXU_DTYPE)
    o_ref[0] = jnp.dot(s, w_ref[0].astype(MXU_DTYPE), preferred_element_type=F32) + b_ref[0]


def _ada(cond, w_ada, b_ada):
    depth, d, n = w_ada.shape
    tn = 1024
    blocks = _nbytes((COND_ROWS, d), F32) + _nbytes((d, tn), F32) + _nbytes((COND_ROWS, tn), F32)
    return pl.pallas_call(
        _ada_kernel,
        grid=(depth, n // tn),
        in_specs=[pl.BlockSpec((COND_ROWS, d), lambda l, j: (0, 0)),
                  pl.BlockSpec((1, d, tn), lambda l, j: (l, 0, j)),
                  pl.BlockSpec((1, 1, tn), lambda l, j: (l, 0, j))],
        out_specs=pl.BlockSpec((1, COND_ROWS, tn), lambda l, j: (l, 0, j)),
        out_shape=jax.ShapeDtypeStruct((depth, COND_ROWS, n), F32),
        compiler_params=_params(("parallel", "parallel"), blocks),
        name="ada",
    )(cond, w_ada, b_ada.reshape(depth, 1, n))


def _rmsnorm(x, g):
    return x * lax.rsqrt(jnp.mean(x * x, axis=-1, keepdims=True) + EPS) * g


def _norm_mod_kernel(x_ref, g_ref, sc_ref, sh_ref, o_ref):
    y = _rmsnorm(x_ref[...], g_ref[...]) * (1.0 + sc_ref[0]) + sh_ref[0]
    o_ref[...] = y.astype(o_ref.dtype)


def _norm_kernel(x_ref, g_ref, o_ref):
    o_ref[...] = _rmsnorm(x_ref[...], g_ref[...]).astype(o_ref.dtype)


def _pack_bf16_pair(lo, hi):
    lo_b = pltpu.bitcast(lo.astype(BF16).astype(F32), jnp.uint32)
    hi_b = pltpu.bitcast(hi.astype(BF16).astype(F32), jnp.uint32)
    return (lo_b >> 16) | (hi_b & jnp.uint32(0xFFFF0000))


def _unpack_bf16_pair(p):
    lo = pltpu.bitcast(p << 16, F32)
    hi = pltpu.bitcast(p & jnp.uint32(0xFFFF0000), F32)
    return lo, hi


def _norm_router_kernel(x_ref, g_ref, sc_ref, sh_ref, wr_ref, br_ref, hp_ref, idx_ref, prob_ref):
    h = _rmsnorm(x_ref[...], g_ref[...]) * (1.0 + sc_ref[0]) + sh_ref[0]
    half = h.shape[1] // 2
    hp_ref[...] = _pack_bf16_pair(h[:, :half], h[:, half:])
    logits = lax.dot_general(wr_ref[...], h, (((1,), (1,)), ((), ())),
                             precision=lax.Precision.HIGHEST,
                             preferred_element_type=F32) + br_ref[...]
    e_iota = lax.broadcasted_iota(jnp.int32, logits.shape, 0)
    m1 = jnp.max(logits, axis=0, keepdims=True)
    i1 = jnp.min(jnp.where(logits == m1, e_iota, N_EXPERTS), axis=0, keepdims=True)
    rest = jnp.where(e_iota == i1, -jnp.inf, logits)
    m2 = jnp.max(rest, axis=0, keepdims=True)
    i2 = jnp.min(jnp.where(rest == m2, e_iota, N_EXPERTS), axis=0, keepdims=True)
    r = jnp.exp(m2 - m1)
    p1 = 1.0 / (1.0 + r)
    idx_ref[...] = jnp.concatenate([i1, i2], axis=0)
    prob_ref[...] = jnp.concatenate([p1, r * p1], axis=0)


def _mod_specs(k, tm, n_ctx, lat_len):
    return pl.BlockSpec((1, 1, D_MODEL),
                        lambda i: (_seq_row(i, tm, n_ctx, lat_len) * N_MOD + k, 0, 0))


def _norm_mod(x, g, mod, k_scale, k_shift, n_ctx, lat_len):
    nt, d = x.shape
    blocks = _nbytes((TM, d), F32) + _nbytes((TM, d), BF16)
    return pl.pallas_call(
        _norm_mod_kernel,
        grid=(nt // TM,),
        in_specs=[pl.BlockSpec((TM, d), lambda i: (i, 0)),
                  pl.BlockSpec((1, d), lambda i: (0, 0)),
                  _mod_specs(k_scale, TM, n_ctx, lat_len),
                  _mod_specs(k_shift, TM, n_ctx, lat_len)],
        out_specs=pl.BlockSpec((TM, d), lambda i: (i, 0)),
        out_shape=jax.ShapeDtypeStruct((nt, d), MXU_DTYPE),
        compiler_params=_params(("parallel",), blocks),
        name="norm_mod",
    )(x, g.reshape(1, d), mod, mod)


def _final_norm(x, g):
    nt, d = x.shape
    blocks = 2 * _nbytes((TM, d), F32)
    return pl.pallas_call(
        _norm_kernel,
        grid=(nt // TM,),
        in_specs=[pl.BlockSpec((TM, d), lambda i: (i, 0)),
                  pl.BlockSpec((1, d), lambda i: (0, 0))],
        out_specs=pl.BlockSpec((TM, d), lambda i: (i, 0)),
        out_shape=jax.ShapeDtypeStruct((nt, d), F32),
        compiler_params=_params(("parallel",), blocks),
        name="final_norm",
    )(x, g.reshape(1, d))


def _norm_router(x, g, mod, k_scale, k_shift, w_r, b_r, n_ctx, lat_len):
    nt, d = x.shape
    blocks = _nbytes((TM, d), F32) + _nbytes((TM, d // 2), jnp.uint32)
    return pl.pallas_call(
        _norm_router_kernel,
        grid=(nt // TM,),
        in_specs=[pl.BlockSpec((TM, d), lambda i: (i, 0)),
                  pl.BlockSpec((1, d), lambda i: (0, 0)),
                  _mod_specs(k_scale, TM, n_ctx, lat_len),
                  _mod_specs(k_shift, TM, n_ctx, lat_len),
                  pl.BlockSpec((N_EXPERTS, d), lambda i: (0, 0)),
                  pl.BlockSpec((N_EXPERTS, 1), lambda i: (0, 0))],
        out_specs=[pl.BlockSpec((TM, d // 2), lambda i: (i, 0)),
                   pl.BlockSpec((2, TM), lambda i: (0, i)),
                   pl.BlockSpec((2, TM), lambda i: (0, i))],
        out_shape=[jax.ShapeDtypeStruct((nt, d // 2), jnp.uint32),
                   jax.ShapeDtypeStruct((2, nt), jnp.int32),
                   jax.ShapeDtypeStruct((2, nt), F32)],
        compiler_params=_params(("parallel",), blocks),
        name="norm_router",
    )(x, g.reshape(1, d), mod, mod, w_r.T, b_r.reshape(N_EXPERTS, 1))


def _mm_kernel(x_ref, w_ref, o_ref, *, act):
    y = jnp.dot(x_ref[...], w_ref[...], preferred_element_type=F32)
    if act == "sigmoid":
        y = jax.nn.sigmoid(y)
    o_ref[...] = y.astype(o_ref.dtype)


def _mm(x, w, out_dtype, act=None, tm=1024, tn=512, name="mm"):
    m, k = x.shape
    n = w.shape[1]
    tm = min(tm, m)
    blocks = _nbytes((tm, k), x.dtype) + _nbytes((k, tn), w.dtype) + _nbytes((tm, tn), out_dtype)
    return pl.pallas_call(
        functools.partial(_mm_kernel, act=act),
        grid=(m // tm, n // tn),
        in_specs=[pl.BlockSpec((tm, k), lambda i, j: (i, 0)),
                  pl.BlockSpec((k, tn), lambda i, j: (0, j))],
        out_specs=pl.BlockSpec((tm, tn), lambda i, j: (i, j)),
        out_shape=jax.ShapeDtypeStruct((m, n), out_dtype),
        compiler_params=_params(("parallel", "parallel"), blocks),
        name=name,
    )(x, w)


def _mm_res_kernel(a_ref, w_ref, x_ref, g_ref, o_ref):
    y = jnp.dot(a_ref[...], w_ref[...], preferred_element_type=F32)
    o_ref[...] = x_ref[...] + g_ref[0] * y


def _mm_residual(a, w, x, mod, k_gate, n_ctx, lat_len):
    m, k = a.shape
    n = w.shape[1]
    blocks = _nbytes((TM, k), a.dtype) + _nbytes((k, n), w.dtype) + 2 * _nbytes((TM, n), F32)
    return pl.pallas_call(
        _mm_res_kernel,
        grid=(m // TM,),
        in_specs=[pl.BlockSpec((TM, k), lambda i: (i, 0)),
                  pl.BlockSpec((k, n), lambda i: (0, 0)),
                  pl.BlockSpec((TM, n), lambda i: (i, 0)),
                  _mod_specs(k_gate, TM, n_ctx, lat_len)],
        out_specs=pl.BlockSpec((TM, n), lambda i: (i, 0)),
        out_shape=jax.ShapeDtypeStruct((m, n), F32),
        compiler_params=_params(("parallel",), blocks),
        name="out_proj",
    )(a, w, x, mod)


def _ssm_kernel(u_ref, lf_ref, rf_ref, lb_ref, rb_ref, wst_ref, wout_ref, lam_ref, h0_ref, d_ref,
                y_ref, fin_ref, toep_ref, hin_ref, *, ctx_seqs, ctx_chunks, lat_seqs, lat_chunks):
    kdim = SSM_GROUP * CHUNK
    row_s = lax.broadcasted_iota(jnp.int32, (CHUNK, kdim), 0)
    col_t = lax.broadcasted_iota(jnp.int32, (CHUNK, kdim), 1) & (CHUNK - 1)
    causal = col_t >= row_s
    anti = col_t <= row_s
    rf = rf_ref[0]
    rb = rb_ref[0]

    def build(j, carry):
        rows = pl.ds(pl.multiple_of(j * CHUNK, CHUNK), CHUNK)
        a = jnp.dot(lf_ref[0, rows, :], rf, preferred_element_type=F32)
        b = jnp.dot(lb_ref[0, rows, :], rb, preferred_element_type=F32)
        toep_ref[rows, :] = (jnp.where(causal, a, 0.0) + jnp.where(anti, b, 0.0)).astype(toep_ref.dtype)
        return carry

    lax.fori_loop(0, SSM_GROUP, build, 0)

    u = jnp.concatenate([u_ref[0, j] for j in range(SSM_GROUP)], axis=-1)
    ub = u.astype(MXU_DTYPE)
    s_all = jnp.dot(ub, wst_ref[0], preferred_element_type=F32)

    half = 2 * SSM_STATE
    a_re = lam_ref[0][:, :half]
    a_im = lam_ref[0][:, half:]

    def advance(h, s):
        h_re, h_im = h[:, :half], h[:, half:]
        return jnp.concatenate([a_re * h_re - a_im * h_im + s[:, :half],
                                a_re * h_im + a_im * h_re + s[:, half:]], axis=-1)

    lane = lax.broadcasted_iota(jnp.int32, (1, 2 * half), 1)
    is_fwd = (lane & (half - 1)) < SSM_STATE

    def chain(base, seqs, chunks, h_start):
        rows = lambda c: slice(base + c * seqs, base + (c + 1) * seqs)
        h = h_start
        for c in range(chunks):
            hin_ref[rows(c), :] = h
            h = advance(h, s_all[rows(c), :])
        fwd_end = h
        h = h_start
        for c in reversed(range(chunks)):
            hin_ref[rows(c), :] = jnp.where(is_fwd, hin_ref[rows(c), :], h)
            h = advance(h, s_all[rows(c), :])
        return jnp.where(is_fwd, fwd_end, h)

    fin_ref[0] = chain(0, ctx_seqs, ctx_chunks, jnp.zeros((ctx_seqs, 2 * half), F32))
    chain(ctx_seqs * ctx_chunks, lat_seqs, lat_chunks, h0_ref[0])

    hin = hin_ref[...].astype(MXU_DTYPE)
    nblk = 4
    for q in range(SSM_GROUP // nblk):
        cols = slice(q * nblk * CHUNK, (q + 1) * nblk * CHUNK)
        y = (jnp.dot(ub, toep_ref[:, cols], preferred_element_type=F32)
             + jnp.dot(hin, wout_ref[0, :, cols], preferred_element_type=F32)
             + u[:, cols] * d_ref[0][:, cols])
        for r in range(nblk):
            y_ref[0, q * nblk + r] = y[:, r * CHUNK:(r + 1) * CHUNK]


def _ssm(u_t, tabs, h0, ctx_seqs, ctx_chunks, lat_seqs, lat_chunks):
    g, hh, rows, _ = u_t.shape
    kdim = SSM_GROUP * CHUNK
    ns = 4 * SSM_STATE
    blocks = (2 * _nbytes((hh, rows, CHUNK), F32) + 4 * _nbytes((kdim, 2 * SSM_STATE), MXU_DTYPE)
              + 2 * _nbytes((kdim, ns), MXU_DTYPE))
    scratch = _nbytes((kdim, kdim), MXU_DTYPE) + _nbytes((rows, ns), F32)
    per_g = lambda *shape: pl.BlockSpec((1,) + shape, lambda i: (i,) + (0,) * len(shape))
    return pl.pallas_call(
        functools.partial(_ssm_kernel, ctx_seqs=ctx_seqs, ctx_chunks=ctx_chunks,
                          lat_seqs=lat_seqs, lat_chunks=lat_chunks),
        grid=(g,),
        in_specs=[per_g(hh, rows, CHUNK),
                  per_g(kdim, 2 * SSM_STATE), per_g(2 * SSM_STATE, kdim),
                  per_g(kdim, 2 * SSM_STATE), per_g(2 * SSM_STATE, kdim),
                  per_g(kdim, ns), per_g(ns, kdim),
                  per_g(1, ns), per_g(lat_seqs, ns), per_g(1, kdim)],
        out_specs=[per_g(hh, rows, CHUNK), per_g(ctx_seqs, ns)],
        out_shape=[jax.ShapeDtypeStruct(u_t.shape, F32),
                   jax.ShapeDtypeStruct((g, ctx_seqs, ns), F32)],
        scratch_shapes=[pltpu.VMEM((kdim, kdim), MXU_DTYPE), pltpu.VMEM((rows, ns), F32)],
        compiler_params=_params(("parallel",), blocks, scratch),
        name="s5_chunked",
    )(u_t, tabs["lf"], tabs["rf"], tabs["lb"], tabs["rb"], tabs["wst"], tabs["wout"],
      tabs["lam"], h0, tabs["d"])


def _cmul(ar, ai, br, bi):
    return ar * br - ai * bi, ar * bi + ai * br


def _ssm_tables(lam_re, lam_im, b_re, b_im, c_re, c_im, log_dt, d_skip):
    dt = jnp.exp(log_dt)[..., None]
    ea, eb = lam_re * dt, lam_im * dt
    mag = jnp.exp(ea)
    lbar_re, lbar_im = mag * jnp.cos(eb), mag * jnp.sin(eb)
    den = lam_re * lam_re + lam_im * lam_im
    nr, ni = lbar_re - 1.0, lbar_im
    coef_re = (nr * lam_re + ni * lam_im) / den
    coef_im = (ni * lam_re - nr * lam_im) / den
    bb_re, bb_im = _cmul(coef_re[..., None], coef_im[..., None], b_re, b_im)

    def power(n):
        n = n.astype(F32)[None, None, :, None]
        m = jnp.exp(ea[:, :, None, :] * n)
        return m * jnp.cos(eb[:, :, None, :] * n), m * jnp.sin(eb[:, :, None, :] * n)

    pos = jnp.arange(CHUNK)
    g, p, h = SSM_GROUPS, SSM_STATE, SSM_GROUP

    def left(dr, n):
        pr, pi = power(n)
        re, im = _cmul(pr[dr][:, None], pi[dr][:, None],
                       jnp.swapaxes(bb_re[dr], 1, 2)[:, :, None, :], jnp.swapaxes(bb_im[dr], 1, 2)[:, :, None, :])
        return re.reshape(g, h * CHUNK, p), im.reshape(g, h * CHUNK, p)

    def right(dr, n):
        pr, pi = power(n)
        re, im = _cmul(jnp.swapaxes(pr[dr], 1, 2)[:, :, None, :], jnp.swapaxes(pi[dr], 1, 2)[:, :, None, :],
                       jnp.swapaxes(c_re[dr], 1, 2)[:, :, :, None], jnp.swapaxes(c_im[dr], 1, 2)[:, :, :, None])
        return re.reshape(g, p, h * CHUNK), im.reshape(g, p, h * CHUNK)

    cat_l = lambda re, im: jnp.concatenate([re, -im], axis=-1).astype(MXU_DTYPE)
    cat_r = lambda re, im: jnp.concatenate([re, im], axis=1).astype(MXU_DTYPE)
    ctr = SSM_CENTER
    tabs = {
        "lf": cat_l(*left(0, ctr - pos)), "rf": cat_r(*right(0, pos - ctr)),
        "lb": cat_l(*left(1, pos - ctr)), "rb": cat_r(*right(1, ctr - pos)),
    }
    sf_re, sf_im = left(0, CHUNK - 1 - pos)
    sb_re, sb_im = left(1, pos)
    tabs["wst"] = jnp.concatenate([sf_re, sb_re, sf_im, sb_im], axis=-1).astype(MXU_DTYPE)
    of_re, of_im = right(0, pos + 1)
    ob_re, ob_im = right(1, CHUNK - pos)
    tabs["wout"] = jnp.concatenate([of_re, ob_re, -of_im, -ob_im], axis=1).astype(MXU_DTYPE)
    pt_re, pt_im = power(jnp.array([CHUNK]))
    tabs["lam"] = jnp.concatenate([pt_re[0], pt_re[1], pt_im[0], pt_im[1]], axis=-1)
    tabs["d"] = jnp.repeat(d_skip.reshape(g, 1, h), CHUNK, axis=-1)
    return tabs


def _to_group_major(z_ssm, seqs, chunks):
    v = z_ssm.reshape(seqs, chunks, CHUNK, SSM_GROUPS, SSM_GROUP)
    return v.transpose(3, 4, 1, 0, 2).reshape(SSM_GROUPS, SSM_GROUP, chunks * seqs, CHUNK)


def _from_group_major(y, seqs, chunks):
    v = y.reshape(SSM_GROUPS, SSM_GROUP, chunks, seqs, CHUNK)
    return v.transpose(3, 2, 4, 0, 1).reshape(seqs * chunks * CHUNK, BRANCH_W)


def _glu_kernel(y_ref, w_ref, o_ref):
    z = jax.nn.gelu(y_ref[...])
    gate = jax.nn.sigmoid(jnp.dot(z.astype(MXU_DTYPE), w_ref[...], preferred_element_type=F32))
    o_ref[...] = (z * gate).astype(o_ref.dtype)


def _glu(y, w):
    nt, w_ = y.shape
    blocks = _nbytes((TM, w_), F32) + _nbytes((w_, w_), MXU_DTYPE) + _nbytes((TM, w_), MXU_DTYPE)
    return pl.pallas_call(
        _glu_kernel,
        grid=(nt // TM,),
        in_specs=[pl.BlockSpec((TM, w_), lambda i: (i, 0)),
                  pl.BlockSpec((w_, w_), lambda i: (0, 0))],
        out_specs=pl.BlockSpec((TM, w_), lambda i: (i, 0)),
        out_shape=jax.ShapeDtypeStruct((nt, w_), MXU_DTYPE),
        compiler_params=_params(("parallel",), blocks),
        name="s5_glu",
    )(y, w)


def _fft_kernel(x_ref, cs_ref, dft_ref, o_ref, pq_ref):
    seq_len = x_ref.shape[0]

    @pl.when(pl.program_id(1) == 0)
    def _():
        pq = jnp.dot(x_ref[...].astype(MXU_DTYPE), cs_ref[...], preferred_element_type=F32)
        pq_ref[:seq_len, :] = pq[:, :BRANCH_W].astype(pq_ref.dtype)
        pq_ref[seq_len:, :] = pq[:, BRANCH_W:].astype(pq_ref.dtype)

    o_ref[...] = jnp.dot(dft_ref[...], pq_ref[...], preferred_element_type=F32).astype(o_ref.dtype)


def _dft_tables(seq_len):
    def cs(n):
        k = jnp.arange(n, dtype=jnp.int32)
        ang = ((k[:, None] * k[None, :]) % n).astype(F32) * (2.0 * math.pi / n)
        return jnp.cos(ang) / math.sqrt(n), jnp.sin(ang) / math.sqrt(n)

    cl, sl = cs(seq_len)
    cc, sc = cs(CHUNK)
    eye = jnp.eye(BRANCH_W // CHUNK, dtype=F32)
    chan = jnp.concatenate([jnp.kron(eye, cc), jnp.kron(eye, sc)], axis=1)
    return jnp.concatenate([cl, -sl], axis=1).astype(MXU_DTYPE), chan.astype(MXU_DTYPE)


def _fft(z, row0, seqs, seq_len):
    dft, chan = _dft_tables(seq_len)
    tm = min(seq_len, TM)
    blocks = (_nbytes((seq_len, BRANCH_W), F32) + _nbytes(chan.shape, MXU_DTYPE)
              + _nbytes((tm, 2 * seq_len), MXU_DTYPE) + _nbytes((tm, BRANCH_W), MXU_DTYPE))
    scratch = _nbytes((2 * seq_len, BRANCH_W), MXU_DTYPE)
    seq0 = row0 // seq_len
    return pl.pallas_call(
        _fft_kernel,
        grid=(seqs, seq_len // tm),
        in_specs=[pl.BlockSpec((seq_len, BRANCH_W), lambda b, j: (seq0 + b, 1)),
                  pl.BlockSpec(chan.shape, lambda b, j: (0, 0)),
                  pl.BlockSpec((tm, 2 * seq_len), lambda b, j: (j, 0))],
        out_specs=pl.BlockSpec((tm, BRANCH_W), lambda b, j: (b * (seq_len // tm) + j, 0)),
        out_shape=jax.ShapeDtypeStruct((seqs * seq_len, BRANCH_W), MXU_DTYPE),
        scratch_shapes=[pltpu.VMEM((2 * seq_len, BRANCH_W), MXU_DTYPE)],
        compiler_params=_params(("parallel", "arbitrary"), blocks, scratch),
        name="fnet_dft",
    )(z, chan, dft)


def _sgu_kernel(z_ref, g_ref, ws_ref, bs_ref, o_ref):
    uv = jax.nn.gelu(z_ref[...])
    u = uv[:, :BRANCH_W]
    v = _rmsnorm(uv[:, BRANCH_W:], g_ref[...]).astype(MXU_DTYPE)
    head_w = BRANCH_W // SGU_HEADS
    for c in range(z_ref.shape[0] // CHUNK):
        rows = slice(c * CHUNK, (c + 1) * CHUNK)
        s = jnp.concatenate(
            [jnp.dot(ws_ref[h], v[rows, h * head_w:(h + 1) * head_w], preferred_element_type=F32)
             for h in range(SGU_HEADS)], axis=-1)
        o_ref[rows, :] = (u[rows, :] * (s + bs_ref[...])).astype(o_ref.dtype)


def _sgu(z, g, w_s, b_s):
    nt = z.shape[0]
    tm = 2 * CHUNK
    head_w = BRANCH_W // SGU_HEADS
    bias = jnp.repeat(b_s.T, head_w, axis=1)
    blocks = _nbytes((tm, 2 * BRANCH_W), F32) + _nbytes((tm, BRANCH_W), MXU_DTYPE)
    return pl.pallas_call(
        _sgu_kernel,
        grid=(nt // tm,),
        in_specs=[pl.BlockSpec((tm, 2 * BRANCH_W), lambda i: (i, 1)),
                  pl.BlockSpec((1, BRANCH_W), lambda i: (0, 0)),
                  pl.BlockSpec((SGU_HEADS, CHUNK, CHUNK), lambda i: (0, 0, 0)),
                  pl.BlockSpec((CHUNK, BRANCH_W), lambda i: (0, 0))],
        out_specs=pl.BlockSpec((tm, BRANCH_W), lambda i: (i, 0)),
        out_shape=jax.ShapeDtypeStruct((nt, BRANCH_W), MXU_DTYPE),
        compiler_params=_params(("parallel",), blocks),
        name="sgu",
    )(z, g.reshape(1, BRANCH_W), w_s.astype(MXU_DTYPE), bias)


def _conv_kernel(x_ref, b_ref, c_ref, w_ref, o_ref, *, n_ctx_tiles, ctx_period, lat_period):
    t = c_ref[...] * x_ref[...]
    tm = t.shape[0]
    period = jnp.where(pl.program_id(0) < n_ctx_tiles, ctx_period, lat_period)
    pos = lax.broadcasted_iota(jnp.int32, t.shape, 0) & (period - 1)
    prev = jnp.where(pos == 0, 0.0, pltpu.roll(t, 1, axis=0))
    nxt = jnp.where(pos == period - 1, 0.0, pltpu.roll(t, tm - 1, axis=0))
    w = w_ref[...]
    o_ref[...] = (b_ref[...] * (prev * w[0:1] + t * w[1:2] + nxt * w[2:3])).astype(o_ref.dtype)


def _conv(z, w, n_ctx, ctx_period):
    nt = z.shape[0]
    col = lambda k: pl.BlockSpec((TM, BRANCH_W), lambda i: (i, k))
    blocks = 3 * _nbytes((TM, BRANCH_W), F32) + _nbytes((TM, BRANCH_W), MXU_DTYPE)
    return pl.pallas_call(
        functools.partial(_conv_kernel, n_ctx_tiles=n_ctx // TM, ctx_period=ctx_period, lat_period=GRID_W),
        grid=(nt // TM,),
        in_specs=[col(4), col(5), col(6), pl.BlockSpec((8, BRANCH_W), lambda i: (0, 0))],
        out_specs=pl.BlockSpec((TM, BRANCH_W), lambda i: (i, 0)),
        out_shape=jax.ShapeDtypeStruct((nt, BRANCH_W), MXU_DTYPE),
        compiler_params=_params(("parallel",), blocks),
        name="short_conv",
    )(z, z, z, jnp.pad(w.T, ((0, 5), (0, 0))))


def _merge_kernel(b0_ref, b1_ref, b2_ref, b3_ref, gate_ref, wup_ref, o_ref):
    acc = None
    for n, b_ref in enumerate((b0_ref, b1_ref, b2_ref, b3_ref)):
        up = jnp.dot(b_ref[...], wup_ref[n], preferred_element_type=F32)
        term = gate_ref[:, n * D_MODEL:(n + 1) * D_MODEL].astype(F32) * up
        acc = term if acc is None else acc + term
    o_ref[...] = acc.astype(o_ref.dtype)


def _merge(branches, gates, w_up):
    nt = gates.shape[0]
    blocks = (4 * _nbytes((TM, BRANCH_W), MXU_DTYPE) + _nbytes((TM, N_BRANCH * D_MODEL), MXU_DTYPE)
              + _nbytes(w_up.shape, MXU_DTYPE) + _nbytes((TM, D_MODEL), MXU_DTYPE))
    br = pl.BlockSpec((TM, BRANCH_W), lambda i: (i, 0))
    return pl.pallas_call(
        _merge_kernel,
        grid=(nt // TM,),
        in_specs=[br, br, br, br,
                  pl.BlockSpec((TM, N_BRANCH * D_MODEL), lambda i: (i, 0)),
                  pl.BlockSpec(w_up.shape, lambda i: (0, 0, 0))],
        out_specs=pl.BlockSpec((TM, D_MODEL), lambda i: (i, 0)),
        out_shape=jax.ShapeDtypeStruct((nt, D_MODEL), MXU_DTYPE),
        compiler_params=_params(("parallel",), blocks),
        name="branch_merge",
    )(*branches, gates, w_up)


def _ffn_kernel(h_ref, w1_ref, w3_ref, w2_ref, x_ref, g_ref, o_ref, acc_ref):
    f = pl.program_id(1)
    h = h_ref[...]
    a = jnp.dot(h, w1_ref[...], preferred_element_type=F32)
    b = jnp.dot(h, w3_ref[...], preferred_element_type=F32)
    act = (a * jax.nn.sigmoid(a) * b).astype(MXU_DTYPE)
    part = jnp.dot(act, w2_ref[...], preferred_element_type=F32)

    @pl.when(f == 0)
    def _():
        acc_ref[...] = part

    @pl.when(f > 0)
    def _():
        acc_ref[...] += part

    @pl.when(f == pl.num_programs(1) - 1)
    def _():
        o_ref[...] = x_ref[...] + g_ref[0] * acc_ref[...]


def _ffn(h, w1, w3, w2, x, mod, k_gate, n_ctx, lat_len):
    nt, d = h.shape
    ff = w1.shape[1]
    blocks = (_nbytes((TM, d), MXU_DTYPE) + 3 * _nbytes((d, TF), MXU_DTYPE) + 2 * _nbytes((TM, d), F32))
    scratch = _nbytes((TM, d), F32)
    return pl.pallas_call(
        _ffn_kernel,
        grid=(nt // TM, ff // TF),
        in_specs=[pl.BlockSpec((TM, d), lambda i, f: (i, 0)),
                  pl.BlockSpec((d, TF), lambda i, f: (0, f)),
                  pl.BlockSpec((d, TF), lambda i, f: (0, f)),
                  pl.BlockSpec((TF, d), lambda i, f: (f, 0)),
                  pl.BlockSpec((TM, d), lambda i, f: (i, 0)),
                  pl.BlockSpec((1, 1, d), lambda i, f: (_seq_row(i, TM, n_ctx, lat_len) * N_MOD + k_gate, 0, 0))],
        out_specs=pl.BlockSpec((TM, d), lambda i, f: (i, 0)),
        out_shape=jax.ShapeDtypeStruct((nt, d), F32),
        scratch_shapes=[pltpu.VMEM((TM, d), F32)],
        compiler_params=_params(("parallel", "arbitrary"), blocks, scratch),
        name="ffn_dense",
    )(h, w1, w3, w2, x, mod)


def _gather_kernel(idx_ref, src_ref, o_ref, sem):
    n = o_ref.shape[0]

    def row_copy(r, src_row):
        return pltpu.make_async_copy(src_ref.at[pl.ds(src_row, 1)], o_ref.at[pl.ds(r, 1)], sem)

    def start(r, carry):
        row_copy(r, idx_ref[r]).start()
        return carry

    def wait(r, carry):
        row_copy(r, 0).wait()
        return carry

    lax.fori_loop(0, n, start, 0)
    lax.fori_loop(0, n, wait, 0)


def _gather_rows(src, idx):
    n = idx.shape[0]
    width = src.shape[1]
    return pl.pallas_call(
        _gather_kernel,
        grid=(n // GATHER_ROWS,),
        in_specs=[pl.BlockSpec((GATHER_ROWS,), lambda i: (i,), memory_space=pltpu.SMEM),
                  pl.BlockSpec(memory_space=pl.ANY)],
        out_specs=pl.BlockSpec((GATHER_ROWS, width), lambda i: (i, 0)),
        out_shape=jax.ShapeDtypeStruct((n, width), src.dtype),
        scratch_shapes=[pltpu.SemaphoreType.DMA(())],
        compiler_params=_params(("arbitrary",), _nbytes((GATHER_ROWS, width), src.dtype)),
        name="moe_gather",
    )(idx, src)


def _moe_ffn_kernel(te_ref, nv_ref, hp_ref, w1_ref, w3_ref, w2_ref, p_ref, o_ref, h_ref, acc_ref):
    i = pl.program_id(0)
    f = pl.program_id(1)

    @pl.when(i < nv_ref[0])
    def _():
        @pl.when(f == 0)
        def _():
            lo, hi = _unpack_bf16_pair(hp_ref[...])
            half = lo.shape[1]
            h_ref[:, :half] = lo.astype(h_ref.dtype)
            h_ref[:, half:] = hi.astype(h_ref.dtype)

        h = h_ref[...]
        a = jnp.dot(h, w1_ref[0], preferred_element_type=F32)
        b = jnp.dot(h, w3_ref[0], preferred_element_type=F32)
        act = (a * jax.nn.sigmoid(a) * b).astype(MXU_DTYPE)
        part = jnp.dot(act, w2_ref[0], preferred_element_type=F32)

        @pl.when(f == 0)
        def _():
            acc_ref[...] = part

        @pl.when(f > 0)
        def _():
            acc_ref[...] += part

        @pl.when(f == pl.num_programs(1) - 1)
        def _():
            d = acc_ref.shape[1]
            o_ref[...] = acc_ref[...] * jnp.tile(p_ref[...], (1, d // p_ref.shape[1]))

    @pl.when(jnp.logical_and(i >= nv_ref[0], f == pl.num_programs(1) - 1))
    def _():
        o_ref[...] = jnp.zeros_like(o_ref)


def _moe_ffn(hp_sorted, w1, w3, w2, p_rows, tile_expert, n_valid):
    r, half = hp_sorted.shape
    d = 2 * half
    ff = w1.shape[2]
    lanes = p_rows.shape[1]
    blocks = (_nbytes((MOE_TM, half), jnp.uint32) + 3 * _nbytes((d, TF), MXU_DTYPE)
              + _nbytes((MOE_TM, lanes), F32) + _nbytes((MOE_TM, d), F32))
    scratch = _nbytes((MOE_TM, d), MXU_DTYPE) + _nbytes((MOE_TM, d), F32)

    def live(i, f, nv):
        return jnp.where(i < nv[0], f, 0)

    grid_spec = pltpu.PrefetchScalarGridSpec(
        num_scalar_prefetch=2,
        grid=(r // MOE_TM, ff // TF),
        in_specs=[pl.BlockSpec((MOE_TM, half), lambda i, f, te, nv: (i, 0)),
                  pl.BlockSpec((1, d, TF), lambda i, f, te, nv: (te[i], 0, live(i, f, nv))),
                  pl.BlockSpec((1, d, TF), lambda i, f, te, nv: (te[i], 0, live(i, f, nv))),
                  pl.BlockSpec((1, TF, d), lambda i, f, te, nv: (te[i], live(i, f, nv), 0)),
                  pl.BlockSpec((MOE_TM, lanes), lambda i, f, te, nv: (i, 0))],
        out_specs=pl.BlockSpec((MOE_TM, d), lambda i, f, te, nv: (i, 0)),
        scratch_shapes=[pltpu.VMEM((MOE_TM, d), MXU_DTYPE), pltpu.VMEM((MOE_TM, d), F32)],
    )
    return pl.pallas_call(
        _moe_ffn_kernel,
        grid_spec=grid_spec,
        out_shape=jax.ShapeDtypeStruct((r, d), F32),
        compiler_params=_params(("arbitrary", "arbitrary"), blocks, scratch),
        name="moe_ffn",
    )(tile_expert, n_valid, hp_sorted, w1, w3, w2, p_rows)


def _combine_kernel(idx_ref, ys_ref, x_ref, g_ref, o_ref, buf_ref, sem):
    n = o_ref.shape[0]

    def row_copy(k, r, src_row):
        return pltpu.make_async_copy(ys_ref.at[pl.ds(src_row, 1)], buf_ref.at[k, pl.ds(r, 1)], sem)

    def start(r, carry):
        row_copy(0, r, idx_ref[0, r]).start()
        row_copy(1, r, idx_ref[1, r]).start()
        return carry

    def wait(r, carry):
        row_copy(0, r, 0).wait()
        row_copy(1, r, 0).wait()
        return carry

    lax.fori_loop(0, n, start, 0)
    lax.fori_loop(0, n, wait, 0)
    o_ref[...] = x_ref[...] + g_ref[0] * (buf_ref[0] + buf_ref[1])


def _combine(ys, dest, x, mod, k_gate, n_ctx, lat_len):
    nt, d = x.shape
    tc = GATHER_ROWS
    blocks = 2 * _nbytes((tc, d), F32)
    scratch = _nbytes((2, tc, d), F32)
    return pl.pallas_call(
        _combine_kernel,
        grid=(nt // tc,),
        in_specs=[pl.BlockSpec((2, tc), lambda i: (0, i), memory_space=pltpu.SMEM),
                  pl.BlockSpec(memory_space=pl.ANY),
                  pl.BlockSpec((tc, d), lambda i: (i, 0)),
                  _mod_specs(k_gate, tc, n_ctx, lat_len)],
        out_specs=pl.BlockSpec((tc, d), lambda i: (i, 0)),
        out_shape=jax.ShapeDtypeStruct((nt, d), F32),
        scratch_shapes=[pltpu.VMEM((2, tc, d), F32), pltpu.SemaphoreType.DMA(())],
        compiler_params=_params(("arbitrary",), blocks, scratch),
        name="moe_combine",
    )(dest, ys, x, mod)


def _route(idx, prob, n_rows):
    nt = idx.shape[1]
    e_flat = idx.reshape(-1)
    onehot = (e_flat[:, None] == jnp.arange(N_EXPERTS)[None, :]).astype(jnp.int32)
    counts = jnp.sum(onehot, axis=0)
    rank = jnp.take_along_axis(jnp.cumsum(onehot, axis=0) - onehot, e_flat[:, None], axis=1)[:, 0]
    padded = ((counts + MOE_TM - 1) // MOE_TM) * MOE_TM
    ends = jnp.cumsum(padded)
    dest = (ends - padded)[e_flat] + rank
    token = jnp.tile(jnp.arange(nt, dtype=jnp.int32), 2)
    src = jnp.zeros((n_rows,), jnp.int32).at[dest].set(token)
    weight = jnp.zeros((n_rows,), F32).at[dest].set(prob.reshape(-1))
    tile_start = jnp.arange(n_rows // MOE_TM, dtype=jnp.int32) * MOE_TM
    tile_expert = jnp.minimum(jnp.sum((tile_start[:, None] >= ends[None, :]).astype(jnp.int32), axis=1),
                              N_EXPERTS - 1).astype(jnp.int32)
    n_valid = (ends[-1] // MOE_TM).astype(jnp.int32).reshape(1)
    return dest.reshape(2, nt).astype(jnp.int32), src, weight, tile_expert, n_valid


def _moe(x, g, mod, w_r, b_r, w1, w3, w2, n_ctx, lat_len):
    nt = x.shape[0]
    hp, idx, prob = _norm_router(x, g, mod, 4, 3, w_r, b_r, n_ctx, lat_len)
    n_rows = 2 * nt + N_EXPERTS * MOE_TM
    dest, src, weight, tile_expert, n_valid = _route(idx, prob, n_rows)
    hp_sorted = _gather_rows(hp, src)
    p_rows = jnp.broadcast_to(weight[:, None], (n_rows, 128))
    ys = _moe_ffn(hp_sorted, w1, w3, w2, p_rows, tile_expert, n_valid)
    return _combine(ys, dest, x, mod, 5, n_ctx, lat_len)


def _mixer(x, mod, lp, h0, dims):
    n_ctx, ctx_seqs, ctx_len, lat_seqs, lat_len = dims
    h = _norm_mod(x, lp["norm1"], mod, 1, 0, n_ctx, lat_len)
    z = _mm(h, lp["w_in"], F32, name="in_proj")
    gates = _mm(h, lp["w_gate"], MXU_DTYPE, act="sigmoid", name="gate_proj")

    ctx_chunks, lat_chunks = ctx_len // CHUNK, lat_len // CHUNK
    z_ssm = z[:, :BRANCH_W]
    u_t = jnp.concatenate([_to_group_major(z_ssm[:n_ctx], ctx_seqs, ctx_chunks),
                           _to_group_major(z_ssm[n_ctx:], lat_seqs, lat_chunks)], axis=2)
    y_t, fin = _ssm(u_t, lp["ssm"], h0, ctx_seqs, ctx_chunks, lat_seqs, lat_chunks)
    n_ctx_rows = ctx_seqs * ctx_chunks
    y_ssm = jnp.concatenate([_from_group_major(y_t[:, :, :n_ctx_rows], ctx_seqs, ctx_chunks),
                             _from_group_major(y_t[:, :, n_ctx_rows:], lat_seqs, lat_chunks)], axis=0)
    br_ssm = _glu(y_ssm, lp["w_glu"])

    br_fft = jnp.concatenate([_fft(z, 0, ctx_seqs, ctx_len), _fft(z, n_ctx, lat_seqs, lat_len)], axis=0)
    br_sgu = _sgu(z, lp["sgu_norm"], lp["w_s"], lp["b_s"])
    br_conv = _conv(z, lp["conv_w"], n_ctx, ctx_len)

    merged = _merge((br_ssm, br_fft, br_sgu, br_conv), gates, lp["w_up"])
    x = _mm_residual(merged, lp["w_o"], x, mod, 2, n_ctx, lat_len)
    return x, fin


def kernel(x_prompt, x_sample, c, state_ssm_re, state_ssm_im, c_ctx, norm1_g, norm2_g, w_ada, b_ada, w_in, w_gate,
           ssm_lam_re, ssm_lam_im, ssm_b_re, ssm_b_im, ssm_c_re, ssm_c_im, ssm_log_dt, ssm_d, ssm_w_glu,
           sgu_norm_g, sgu_w_spatial, sgu_b_spatial, conv_w, w_up, w_o, ffn_w1, ffn_w3, ffn_w2,
           moe_w_router, moe_b_router, moe_w1, moe_w3, moe_w2, final_norm_g):
    ctx_seqs, ctx_len, d = x_prompt.shape
    lat_seqs, lat_len, _ = x_sample.shape
    depth = w_in.shape[0]
    n_ctx = ctx_seqs * ctx_len
    dims = (n_ctx, ctx_seqs, ctx_len, lat_seqs, lat_len)
    assert n_ctx % lat_len == 0 and lat_len % TM == 0 and n_ctx % TM == 0 and 1 + lat_seqs <= COND_ROWS
    assert ctx_len % CHUNK == 0 and ctx_len & (ctx_len - 1) == 0 and lat_len % GRID_W == 0

    x = jnp.concatenate([x_prompt.reshape(n_ctx, d), x_sample.reshape(lat_seqs * lat_len, d)], axis=0)
    cond = jnp.concatenate([c_ctx[None, :], c, jnp.zeros((COND_ROWS - 1 - lat_seqs, d), F32)], axis=0)
    mods = _ada(cond, w_ada, b_ada).reshape(depth, COND_ROWS * N_MOD, 1, d)

    cast = lambda w: w.astype(MXU_DTYPE)
    pad_ff = D_FF_PAD - D_FF
    new_re, new_im = [], []
    for i in range(depth):
        lp = {
            "norm1": norm1_g[i], "w_in": cast(w_in[i]), "w_gate": cast(w_gate[i]),
            "ssm": _ssm_tables(ssm_lam_re[i], ssm_lam_im[i], ssm_b_re[i], ssm_b_im[i], ssm_c_re[i], ssm_c_im[i],
                               ssm_log_dt[i], ssm_d[i]),
            "w_glu": cast(ssm_w_glu[i]), "sgu_norm": sgu_norm_g[i], "w_s": sgu_w_spatial[i],
            "b_s": sgu_b_spatial[i], "conv_w": conv_w[i], "w_up": cast(w_up[i]), "w_o": cast(w_o[i]),
        }
        sr, si = state_ssm_re[:, i], state_ssm_im[:, i]
        h0 = jnp.concatenate([sr[:, 0], sr[:, 1], si[:, 0], si[:, 1]], axis=-1).transpose(1, 0, 2)
        x, fin = _mixer(x, mods[i], lp, h0, dims)
        p = SSM_STATE
        fin = fin.transpose(1, 0, 2)
        new_re.append(jnp.stack([fin[..., :p], fin[..., p:2 * p]], axis=1))
        new_im.append(jnp.stack([fin[..., 2 * p:3 * p], fin[..., 3 * p:]], axis=1))

        j = i // 2
        if i % 2 == 0:
            h2 = _norm_mod(x, norm2_g[i], mods[i], 4, 3, n_ctx, lat_len)
            w1 = cast(jnp.pad(ffn_w1[j], ((0, 0), (0, pad_ff))))
            w3 = cast(jnp.pad(ffn_w3[j], ((0, 0), (0, pad_ff))))
            w2 = cast(jnp.pad(ffn_w2[j], ((0, pad_ff), (0, 0))))
            x = _ffn(h2, w1, w3, w2, x, mods[i], 5, n_ctx, lat_len)
        else:
            x = _moe(x, norm2_g[i], mods[i], moe_w_router[j], moe_b_router[j],
                     cast(moe_w1[j]), cast(moe_w3[j]), cast(moe_w2[j]), n_ctx, lat_len)

    y = _final_norm(x, final_norm_g)
    y_prompt = y[:n_ctx].reshape(ctx_seqs, ctx_len, d)
    y_sample = y[n_ctx:].reshape(lat_seqs, lat_len, d)
    return y_prompt, y_sample, jnp.stack(new_re, axis=1), jnp.stack(new_im, axis=1)
```

```python
import functools
import math

import jax
import jax.numpy as jnp
from jax import lax
from jax.experimental import pallas as pl
from jax.experimental.pallas import tpu as pltpu

F32 = jnp.float32
BF16 = jnp.bfloat16
U32 = jnp.uint32
MXU_DTYPE = BF16

D_MODEL = 2048
BRANCH_W = 512
N_BRANCH = 4
SSM_GROUP = 16
SSM_GROUPS = 32
SSM_STATE = 64
CHUNK = 128
SSM_CENTER = CHUNK // 2
SGU_HEADS = 4
GRID_W = 64
D_FF = 5504
D_FF_PAD = 5632
N_EXPERTS = 8
N_MOD = 6
EPS = 1e-6
COND_ROWS = 16
LANES = 128
ROW_TILES = D_MODEL // 2 // LANES
OUT_TILES = D_MODEL // LANES

TM = 512
TM_MM = 1024
TF = 512
MOE_TM = 512
MOE_TF = 1024
MOVE_ROWS = 256
VMEM_CAP = 60 * 1024 * 1024


def _params(sem, vmem_bytes):
    limit = int(min(VMEM_CAP, max(vmem_bytes + (12 << 20), 32 << 20)))
    return pltpu.CompilerParams(dimension_semantics=sem, vmem_limit_bytes=limit)


def _nbytes(shape, dtype):
    return math.prod(shape) * jnp.dtype(dtype).itemsize


def _seq_row(i, tm, n_ctx, lat_len):
    start = i * tm
    return jnp.where(start < n_ctx, 0, 1 + (start - n_ctx) // lat_len)


def _mod_spec(layer, k, tm, n_ctx, lat_len, grid_rank=1):
    def index(i, *_):
        return (layer, _seq_row(i, tm, n_ctx, lat_len) * N_MOD + k, 0, 0)
    return pl.BlockSpec((1, 1, 1, D_MODEL), index)


def _swap_halves(x):
    return pltpu.roll(x, SSM_STATE, axis=1)


def _ada_kernel(c_ref, w_ref, b_ref, o_ref):
    c = c_ref[...]
    s = (c * jax.nn.sigmoid(c)).astype(MXU_DTYPE)
    o_ref[0] = jnp.dot(s, w_ref[0].astype(MXU_DTYPE), preferred_element_type=F32) + b_ref[0]


def _ada(cond, w_ada, b_ada):
    depth, d, n = w_ada.shape
    tn = 1024
    vmem = 2 * (_nbytes((COND_ROWS, d), F32) + _nbytes((d, tn), F32) + _nbytes((COND_ROWS, tn), F32))
    return pl.pallas_call(
        _ada_kernel,
        grid=(depth, n // tn),
        in_specs=[pl.BlockSpec((COND_ROWS, d), lambda l, j: (0, 0)),
                  pl.BlockSpec((1, d, tn), lambda l, j: (l, 0, j)),
                  pl.BlockSpec((1, 1, tn), lambda l, j: (l, 0, j))],
        out_specs=pl.BlockSpec((1, COND_ROWS, tn), lambda l, j: (l, 0, j)),
        out_shape=jax.ShapeDtypeStruct((depth, COND_ROWS, n), F32),
        compiler_params=_params(("parallel", "parallel"), vmem),
        name="ada",
    )(cond, w_ada, b_ada.reshape(depth, 1, n))


def _rmsnorm(x, g):
    return x * lax.rsqrt(jnp.mean(x * x, axis=-1, keepdims=True) + EPS) * g


def _norm_mod_kernel(x_ref, g_ref, sc_ref, sh_ref, o_ref):
    y = _rmsnorm(x_ref[...], g_ref[0]) * (1.0 + sc_ref[0, 0]) + sh_ref[0, 0]
    o_ref[...] = y.astype(o_ref.dtype)


def _norm_kernel(x_ref, g_ref, o_ref):
    o_ref[...] = _rmsnorm(x_ref[...], g_ref[...]).astype(o_ref.dtype)


def _pack_bf16_pair(lo, hi):
    lo_b = pltpu.bitcast(lo.astype(BF16).astype(F32), U32)
    hi_b = pltpu.bitcast(hi.astype(BF16).astype(F32), U32)
    return (lo_b >> 16) | (hi_b & jnp.uint32(0xFFFF0000))


def _unpack_bf16_pair(p):
    lo = pltpu.bitcast(p << 16, F32)
    hi = pltpu.bitcast(p & jnp.uint32(0xFFFF0000), F32)
    return lo, hi


def _rows_to_tiles(x):
    blocks = jnp.stack([x[:, q * LANES:(q + 1) * LANES] for q in range(x.shape[1] // LANES)], axis=0)
    return pltpu.einshape("qrl->rql", blocks)


def _tiles_to_blocks(t):
    return pltpu.einshape("rql->qrl", t)


def _norm_router_kernel(x_ref, g_ref, sc_ref, sh_ref, wr_ref, br_ref, hp_ref, idx_ref, prob_ref):
    h = _rmsnorm(x_ref[...], g_ref[0]) * (1.0 + sc_ref[0, 0]) + sh_ref[0, 0]
    half = h.shape[1] // 2
    hp_ref[...] = _rows_to_tiles(_pack_bf16_pair(h[:, :half], h[:, half:]))
    logits = lax.dot_general(wr_ref[0], h, (((1,), (1,)), ((), ())),
                             precision=lax.Precision.HIGHEST,
                             preferred_element_type=F32) + br_ref[0]
    e_iota = lax.broadcasted_iota(jnp.int32, logits.shape, 0).astype(F32)
    none = float(N_EXPERTS)
    m1 = jnp.max(logits, axis=0, keepdims=True)
    i1 = jnp.min(jnp.where(logits == m1, e_iota, none), axis=0, keepdims=True)
    rest = jnp.where(e_iota == i1, -jnp.inf, logits)
    m2 = jnp.max(rest, axis=0, keepdims=True)
    i2 = jnp.min(jnp.where(rest == m2, e_iota, none), axis=0, keepdims=True)
    r = jnp.exp(m2 - m1)
    p1 = 1.0 / (1.0 + r)
    idx_ref[...] = jnp.concatenate([i1, i2], axis=0).astype(jnp.int32)
    rows = jnp.concatenate([p1, r * p1, jnp.zeros((LANES - 2, p1.shape[1]), F32)], axis=0)
    prob_ref[...] = rows.T


def _norm_mod(x, g, mods, layer, k_scale, k_shift, n_ctx, lat_len):
    nt, d = x.shape
    vmem = 2 * (_nbytes((TM, d), F32) + _nbytes((TM, d), MXU_DTYPE))
    return pl.pallas_call(
        _norm_mod_kernel,
        grid=(nt // TM,),
        in_specs=[pl.BlockSpec((TM, d), lambda i: (i, 0)),
                  pl.BlockSpec((1, 1, d), lambda i: (layer, 0, 0)),
                  _mod_spec(layer, k_scale, TM, n_ctx, lat_len),
                  _mod_spec(layer, k_shift, TM, n_ctx, lat_len)],
        out_specs=pl.BlockSpec((TM, d), lambda i: (i, 0)),
        out_shape=jax.ShapeDtypeStruct((nt, d), MXU_DTYPE),
        compiler_params=_params(("parallel",), vmem),
        name="norm_mod",
    )(x, g, mods, mods)


def _final_norm(x, g):
    nt, d = x.shape
    vmem = 4 * _nbytes((TM, d), F32)
    return pl.pallas_call(
        _norm_kernel,
        grid=(nt // TM,),
        in_specs=[pl.BlockSpec((TM, d), lambda i: (i, 0)),
                  pl.BlockSpec((1, d), lambda i: (0, 0))],
        out_specs=pl.BlockSpec((TM, d), lambda i: (i, 0)),
        out_shape=jax.ShapeDtypeStruct((nt, d), F32),
        compiler_params=_params(("parallel",), vmem),
        name="final_norm",
    )(x, g.reshape(1, d))


def _norm_router(x, g, mods, layer, k_scale, k_shift, w_r_t, b_r, j, n_ctx, lat_len):
    nt, d = x.shape
    vmem = 2 * (_nbytes((TM, d), F32) + _nbytes((TM, d // 2), U32) + _nbytes((TM, LANES), F32))
    return pl.pallas_call(
        _norm_router_kernel,
        grid=(nt // TM,),
        in_specs=[pl.BlockSpec((TM, d), lambda i: (i, 0)),
                  pl.BlockSpec((1, 1, d), lambda i: (layer, 0, 0)),
                  _mod_spec(layer, k_scale, TM, n_ctx, lat_len),
                  _mod_spec(layer, k_shift, TM, n_ctx, lat_len),
                  pl.BlockSpec((1, N_EXPERTS, d), lambda i: (j, 0, 0)),
                  pl.BlockSpec((1, N_EXPERTS, 1), lambda i: (j, 0, 0))],
        out_specs=[pl.BlockSpec((TM, ROW_TILES, LANES), lambda i: (i, 0, 0)),
                   pl.BlockSpec((2, TM), lambda i: (0, i)),
                   pl.BlockSpec((TM, LANES), lambda i: (i, 0))],
        out_shape=[jax.ShapeDtypeStruct((nt, ROW_TILES, LANES), U32),
                   jax.ShapeDtypeStruct((2, nt), jnp.int32),
                   jax.ShapeDtypeStruct((nt, LANES), F32)],
        compiler_params=_params(("parallel",), vmem),
        name="norm_router",
    )(x, g, mods, mods, w_r_t, b_r)


def _mm_kernel(x_ref, w_ref, o_ref, *, act):
    y = jnp.dot(x_ref[...], w_ref[0], preferred_element_type=F32)
    if act == "sigmoid":
        y = jax.nn.sigmoid(y)
    o_ref[...] = y.astype(o_ref.dtype)


def _mm(x, w, layer, out_dtype, tn, act=None, name="mm"):
    m, k = x.shape
    n = w.shape[2]
    vmem = 2 * (_nbytes((TM_MM, k), x.dtype) + _nbytes((k, tn), w.dtype) + _nbytes((TM_MM, tn), out_dtype))
    return pl.pallas_call(
        functools.partial(_mm_kernel, act=act),
        grid=(m // TM_MM, n // tn),
        in_specs=[pl.BlockSpec((TM_MM, k), lambda i, j: (i, 0)),
                  pl.BlockSpec((1, k, tn), lambda i, j: (layer, 0, j))],
        out_specs=pl.BlockSpec((TM_MM, tn), lambda i, j: (i, j)),
        out_shape=jax.ShapeDtypeStruct((m, n), out_dtype),
        compiler_params=_params(("parallel", "parallel"), vmem),
        name=name,
    )(x, w)


def _mm_nt_kernel(w_ref, x_ref, o_ref):
    o_ref[...] = lax.dot_general(w_ref[0], x_ref[...], (((1,), (1,)), ((), ())),
                                 preferred_element_type=F32)


def _mm_nt(w_t, x, layer):
    m, k = x.shape
    n = w_t.shape[1]
    vmem = 2 * (_nbytes((TM_MM, k), x.dtype) + _nbytes((n, k), w_t.dtype) + _nbytes((n, TM_MM), F32))
    return pl.pallas_call(
        _mm_nt_kernel,
        grid=(m // TM_MM,),
        in_specs=[pl.BlockSpec((1, n, k), lambda i: (layer, 0, 0)),
                  pl.BlockSpec((TM_MM, k), lambda i: (i, 0))],
        out_specs=pl.BlockSpec((n, TM_MM), lambda i: (0, i)),
        out_shape=jax.ShapeDtypeStruct((n, m), F32),
        compiler_params=_params(("parallel",), vmem),
        name="ssm_proj",
    )(w_t, x)


def _mm_res_kernel(a_ref, w_ref, x_ref, g_ref, o_ref):
    y = jnp.dot(a_ref[...], w_ref[0], preferred_element_type=F32)
    o_ref[...] = x_ref[...] + g_ref[0, 0] * y


def _mm_residual(a, w, layer, x, mods, k_gate, n_ctx, lat_len):
    m, k = a.shape
    n = w.shape[2]
    vmem = 2 * (_nbytes((TM, k), a.dtype) + _nbytes((k, n), w.dtype) + 2 * _nbytes((TM, n), F32))
    return pl.pallas_call(
        _mm_res_kernel,
        grid=(m // TM,),
        in_specs=[pl.BlockSpec((TM, k), lambda i: (i, 0)),
                  pl.BlockSpec((1, k, n), lambda i: (layer, 0, 0)),
                  pl.BlockSpec((TM, n), lambda i: (i, 0)),
                  _mod_spec(layer, k_gate, TM, n_ctx, lat_len)],
        out_specs=pl.BlockSpec((TM, n), lambda i: (i, 0)),
        out_shape=jax.ShapeDtypeStruct((m, n), F32),
        compiler_params=_params(("parallel",), vmem),
        name="out_proj",
    )(a, w, x, mods)


PW_LF, PW_RF, PW_LB, PW_RB, PW_SF, PW_SB, PW_OF, PW_OB = range(8)
V_LF, V_LB, V_RF, V_RB, V_SF, V_SB, V_OF, V_OB = (2 * k for k in range(8))


def _ssm_kernel(u_ref, pw_ref, vec_ref, lam_ref, h0_ref, d_ref, y_ref, fin_ref,
                lf_ref, lb_ref, rf_ref, rb_ref, wst_ref, wout_ref, toep_ref, sf_ref, sb_ref, hf_ref, hb_ref,
                *, ctx_seqs, ctx_chunks, lat_seqs, lat_chunks):
    kdim = SSM_GROUP * CHUNK
    half = 2 * SSM_STATE

    def fill(dst_ref, lanes, table, vset):
        tab = pw_ref[0, 0, table]
        tab_sw = _swap_halves(tab)
        for j in range(SSM_GROUP):
            v1 = vec_ref[0, 0, vset, j:j + 1, :]
            v2 = vec_ref[0, 0, vset + 1, j:j + 1, :]
            dst_ref[j * CHUNK:(j + 1) * CHUNK, lanes] = (tab * v1 + tab_sw * v2).astype(dst_ref.dtype)

    fwd, bwd = slice(0, half), slice(half, 2 * half)
    fill(lf_ref, fwd, PW_LF, V_LF)
    fill(lb_ref, fwd, PW_LB, V_LB)
    fill(rf_ref, fwd, PW_RF, V_RF)
    fill(rb_ref, fwd, PW_RB, V_RB)
    fill(wst_ref, fwd, PW_SF, V_SF)
    fill(wst_ref, bwd, PW_SB, V_SB)
    fill(wout_ref, fwd, PW_OF, V_OF)
    fill(wout_ref, bwd, PW_OB, V_OB)

    row_s = lax.broadcasted_iota(jnp.int32, (CHUNK, kdim), 0)
    col_t = lax.broadcasted_iota(jnp.int32, (CHUNK, kdim), 1) & (CHUNK - 1)
    causal = col_t >= row_s
    anti = col_t <= row_s
    nt_dims = (((1,), (1,)), ((), ()))

    def build(j, carry):
        rows = pl.ds(pl.multiple_of(j * CHUNK, CHUNK), CHUNK)
        a = lax.dot_general(lf_ref[rows, :], rf_ref[...], nt_dims, preferred_element_type=F32)
        b = lax.dot_general(lb_ref[rows, :], rb_ref[...], nt_dims, preferred_element_type=F32)
        toep_ref[rows, :] = (jnp.where(causal, a, 0.0) + jnp.where(anti, b, 0.0)).astype(toep_ref.dtype)
        return carry

    lax.fori_loop(0, SSM_GROUP, build, 0)

    u = jnp.concatenate([u_ref[0, j] for j in range(SSM_GROUP)], axis=-1)
    ub = u.astype(MXU_DTYPE)
    s_all = jnp.dot(ub, wst_ref[...], preferred_element_type=F32)
    sf_ref[...] = s_all[:, fwd]
    sb_ref[...] = s_all[:, bwd]

    def advance(h, s, lanes):
        return h * lam_ref[0, 0, 0:1, lanes] + _swap_halves(h) * lam_ref[0, 0, 1:2, lanes] + s

    def chain(base, seqs, chunks, h_fwd, h_bwd):
        rows = lambda c: pl.ds(base + c, seqs, stride=chunks) if chunks > 1 else pl.ds(base, seqs)
        for c in range(chunks):
            hf_ref[rows(c), :] = h_fwd
            h_fwd = advance(h_fwd, sf_ref[rows(c), :], fwd)
        for c in reversed(range(chunks)):
            hb_ref[rows(c), :] = h_bwd
            h_bwd = advance(h_bwd, sb_ref[rows(c), :], bwd)
        return h_fwd, h_bwd

    zeros = jnp.zeros((ctx_seqs, half), F32)
    end_fwd, end_bwd = chain(0, ctx_seqs, ctx_chunks, zeros, zeros)
    fin_ref[0, :, fwd] = end_fwd
    fin_ref[0, :, bwd] = end_bwd
    chain(ctx_seqs * ctx_chunks, lat_seqs, lat_chunks, h0_ref[0, 0, :, fwd], h0_ref[0, 0, :, bwd])

    hin = jnp.concatenate([hf_ref[...], hb_ref[...]], axis=-1).astype(MXU_DTYPE)
    nblk = 4
    for q in range(SSM_GROUP // nblk):
        cols = slice(q * nblk * CHUNK, (q + 1) * nblk * CHUNK)
        y = (jnp.dot(ub, toep_ref[:, cols], preferred_element_type=F32)
             + lax.dot_general(hin, wout_ref[cols, :], nt_dims, preferred_element_type=F32)
             + u[:, cols] * d_ref[0, 0][:, cols])
        for r in range(nblk):
            y_ref[0, q * nblk + r] = y[:, r * CHUNK:(r + 1) * CHUNK]


def _ssm(u_t, tabs, layer, ctx_seqs, ctx_chunks, lat_seqs, lat_chunks):
    g, hh, rows, _ = u_t.shape
    kdim = SSM_GROUP * CHUNK
    ns = 4 * SSM_STATE
    half = 2 * SSM_STATE
    n_pw, n_vec = tabs["pw"].shape[2], tabs["vec"].shape[2]
    vmem = (2 * (2 * _nbytes((hh, rows, CHUNK), F32) + _nbytes((n_pw, CHUNK, LANES), F32)
                 + _nbytes((n_vec, SSM_GROUP, LANES), F32))
            + 4 * _nbytes((kdim, half), MXU_DTYPE) + 2 * _nbytes((kdim, ns), MXU_DTYPE)
            + _nbytes((kdim, kdim), MXU_DTYPE) + 2 * _nbytes((rows, ns), F32)
            + 3 * _nbytes((rows, kdim), F32))
    per_lg = lambda *shape: pl.BlockSpec((1, 1) + shape, lambda i: (layer, i) + (0,) * len(shape))
    per_g = lambda *shape: pl.BlockSpec((1,) + shape, lambda i: (i,) + (0,) * len(shape))
    return pl.pallas_call(
        functools.partial(_ssm_kernel, ctx_seqs=ctx_seqs, ctx_chunks=ctx_chunks,
                          lat_seqs=lat_seqs, lat_chunks=lat_chunks),
        grid=(g,),
        in_specs=[per_g(hh, rows, CHUNK),
                  per_lg(n_pw, CHUNK, LANES), per_lg(n_vec, SSM_GROUP, LANES),
                  per_lg(2, ns), per_lg(lat_seqs, ns), per_lg(1, kdim)],
        out_specs=[per_g(hh, rows, CHUNK), per_g(ctx_seqs, ns)],
        out_shape=[jax.ShapeDtypeStruct(u_t.shape, F32),
                   jax.ShapeDtypeStruct((g, ctx_seqs, ns), F32)],
        scratch_shapes=[pltpu.VMEM((kdim, half), MXU_DTYPE), pltpu.VMEM((kdim, half), MXU_DTYPE),
                        pltpu.VMEM((kdim, half), MXU_DTYPE), pltpu.VMEM((kdim, half), MXU_DTYPE),
                        pltpu.VMEM((kdim, ns), MXU_DTYPE), pltpu.VMEM((kdim, ns), MXU_DTYPE),
                        pltpu.VMEM((kdim, kdim), MXU_DTYPE),
                        pltpu.VMEM((rows, half), F32), pltpu.VMEM((rows, half), F32),
                        pltpu.VMEM((rows, half), F32), pltpu.VMEM((rows, half), F32)],
        compiler_params=_params(("parallel",), vmem),
        name="s5_chunked",
    )(u_t, tabs["pw"], tabs["vec"], tabs["lam"], tabs["h0"], tabs["d"])


def _cmul(ar, ai, br, bi):
    return ar * br - ai * bi, ar * bi + ai * br


def _ssm_tables(lam_re, lam_im, b_re, b_im, c_re, c_im, log_dt, d_skip, state_re, state_im):
    dt = jnp.exp(log_dt)[..., None]
    ea, eb = lam_re * dt, lam_im * dt
    mag = jnp.exp(ea)
    lbar_re, lbar_im = mag * jnp.cos(eb), mag * jnp.sin(eb)
    den = lam_re * lam_re + lam_im * lam_im
    nr, ni = lbar_re - 1.0, lbar_im
    coef_re = (nr * lam_re + ni * lam_im) / den
    coef_im = (ni * lam_re - nr * lam_im) / den
    bb_re, bb_im = _cmul(coef_re[..., None], coef_im[..., None], b_re, b_im)
    bb_re, bb_im = jnp.swapaxes(bb_re, 3, 4), jnp.swapaxes(bb_im, 3, 4)

    def power(dr, n):
        n = n.astype(F32)[None, None, :, None]
        m = jnp.exp(ea[:, dr, :, None, :] * n)
        ang = eb[:, dr, :, None, :] * n
        return jnp.concatenate([m * jnp.cos(ang), m * jnp.sin(ang)], axis=-1)

    pos = jnp.arange(CHUNK)
    ctr = SSM_CENTER
    pw = jnp.stack([power(0, ctr - pos), power(0, pos - ctr), power(1, pos - ctr), power(1, ctr - pos),
                    power(0, CHUNK - 1 - pos), power(1, pos), power(0, pos + 1), power(1, CHUNK - pos)],
                   axis=2)

    cat = lambda *xs: jnp.concatenate(xs, axis=-1)
    conj_rows = lambda re, im: [cat(re, -re), cat(-im, -im)]
    plain_rows = lambda re, im: [cat(re, re), cat(-im, im)]
    vec = jnp.stack(conj_rows(bb_re[:, 0], bb_im[:, 0]) + conj_rows(bb_re[:, 1], bb_im[:, 1])
                    + plain_rows(c_re[:, 0], c_im[:, 0]) + plain_rows(c_re[:, 1], c_im[:, 1])
                    + plain_rows(bb_re[:, 0], bb_im[:, 0]) + plain_rows(bb_re[:, 1], bb_im[:, 1])
                    + conj_rows(c_re[:, 0], c_im[:, 0]) + conj_rows(c_re[:, 1], c_im[:, 1]),
                   axis=2)

    pt = power(0, jnp.array([CHUNK])), power(1, jnp.array([CHUNK]))
    p = SSM_STATE
    lam = jnp.concatenate(
        [cat(pt[0][..., :p], pt[0][..., :p], pt[1][..., :p], pt[1][..., :p]),
         cat(-pt[0][..., p:], pt[0][..., p:], -pt[1][..., p:], pt[1][..., p:])], axis=2)
    d_rows = jnp.repeat(d_skip.reshape(-1, SSM_GROUPS, 1, SSM_GROUP), CHUNK, axis=-1)
    h0 = cat(state_re[:, :, 0], state_im[:, :, 0], state_re[:, :, 1], state_im[:, :, 1]).transpose(1, 2, 0, 3)
    return {"pw": pw, "vec": vec, "lam": lam, "d": d_rows, "h0": h0}


def _glu_kernel(y_ref, w_ref, o_ref):
    z = jax.nn.gelu(y_ref[...])
    gate = jax.nn.sigmoid(jnp.dot(w_ref[0], z.astype(MXU_DTYPE), preferred_element_type=F32))
    o_ref[...] = (z * gate).T.astype(o_ref.dtype)


def _glu(y_t, w_t, layer):
    w_, nt = y_t.shape
    vmem = 2 * (_nbytes((w_, TM), F32) + _nbytes((w_, w_), MXU_DTYPE) + _nbytes((TM, w_), MXU_DTYPE))
    return pl.pallas_call(
        _glu_kernel,
        grid=(nt // TM,),
        in_specs=[pl.BlockSpec((w_, TM), lambda i: (0, i)),
                  pl.BlockSpec((1, w_, w_), lambda i: (layer, 0, 0))],
        out_specs=pl.BlockSpec((TM, w_), lambda i: (i, 0)),
        out_shape=jax.ShapeDtypeStruct((nt, w_), MXU_DTYPE),
        compiler_params=_params(("parallel",), vmem),
        name="s5_glu",
    )(y_t, w_t)


Z_SGU_BLOCK, Z_FFT_BLOCK, Z_CONV_BLOCK = 0, 2, 3


def _fft_kernel(x_ref, cs_ref, dft_ref, o_ref, pq_ref):
    seq_len = x_ref.shape[0]

    @pl.when(pl.program_id(1) == 0)
    def _():
        pq = jnp.dot(x_ref[...].astype(MXU_DTYPE), cs_ref[...], preferred_element_type=F32)
        pq_ref[:seq_len, :] = pq[:, :BRANCH_W].astype(pq_ref.dtype)
        pq_ref[seq_len:, :] = pq[:, BRANCH_W:].astype(pq_ref.dtype)

    o_ref[...] = jnp.dot(dft_ref[...], pq_ref[...], preferred_element_type=F32).astype(o_ref.dtype)


def _dft_tables(seq_len):
    def cs(n):
        k = jnp.arange(n, dtype=jnp.int32)
        ang = ((k[:, None] * k[None, :]) % n).astype(F32) * (2.0 * math.pi / n)
        return jnp.cos(ang) / math.sqrt(n), jnp.sin(ang) / math.sqrt(n)

    cl, sl = cs(seq_len)
    cc, sc = cs(CHUNK)
    eye = jnp.eye(BRANCH_W // CHUNK, dtype=F32)
    chan = jnp.concatenate([jnp.kron(eye, cc), jnp.kron(eye, sc)], axis=1)
    return jnp.concatenate([cl, -sl], axis=1).astype(MXU_DTYPE), chan.astype(MXU_DTYPE)


def _fft(z, row0, seqs, seq_len):
    dft, chan = _dft_tables(seq_len)
    tm = min(seq_len, TM)
    vmem = (2 * (_nbytes((seq_len, BRANCH_W), F32) + _nbytes(chan.shape, MXU_DTYPE)
                 + _nbytes((tm, 2 * seq_len), MXU_DTYPE) + _nbytes((tm, BRANCH_W), MXU_DTYPE))
            + _nbytes((2 * seq_len, BRANCH_W), MXU_DTYPE) + _nbytes((seq_len, 2 * BRANCH_W), F32))
    seq0 = row0 // seq_len
    return pl.pallas_call(
        _fft_kernel,
        grid=(seqs, seq_len // tm),
        in_specs=[pl.BlockSpec((seq_len, BRANCH_W), lambda b, j: (seq0 + b, Z_FFT_BLOCK)),
                  pl.BlockSpec(chan.shape, lambda b, j: (0, 0)),
                  pl.BlockSpec((tm, 2 * seq_len), lambda b, j: (j, 0))],
        out_specs=pl.BlockSpec((tm, BRANCH_W), lambda b, j: (b * (seq_len // tm) + j, 0)),
        out_shape=jax.ShapeDtypeStruct((seqs * seq_len, BRANCH_W), MXU_DTYPE),
        scratch_shapes=[pltpu.VMEM((2 * seq_len, BRANCH_W), MXU_DTYPE)],
        compiler_params=_params(("parallel", "arbitrary"), vmem),
        name="fnet_dft",
    )(z, chan, dft)


def _sgu_kernel(z_ref, g_ref, ws_ref, bs_ref, o_ref):
    uv = jax.nn.gelu(z_ref[...])
    u = uv[:, :BRANCH_W]
    v = _rmsnorm(uv[:, BRANCH_W:], g_ref[0]).astype(MXU_DTYPE)
    head_w = BRANCH_W // SGU_HEADS
    for c in range(z_ref.shape[0] // CHUNK):
        rows = slice(c * CHUNK, (c + 1) * CHUNK)
        s = jnp.concatenate(
            [jnp.dot(ws_ref[0, h], v[rows, h * head_w:(h + 1) * head_w], preferred_element_type=F32)
             for h in range(SGU_HEADS)], axis=-1)
        o_ref[rows, :] = (u[rows, :] * (s + bs_ref[0])).astype(o_ref.dtype)


def _sgu(z, g, w_s, bias, layer):
    nt = z.shape[0]
    vmem = 2 * (_nbytes((TM, 2 * BRANCH_W), F32) + _nbytes((TM, BRANCH_W), MXU_DTYPE)) + 3 * _nbytes((TM, 2 * BRANCH_W), F32)
    return pl.pallas_call(
        _sgu_kernel,
        grid=(nt // TM,),
        in_specs=[pl.BlockSpec((TM, 2 * BRANCH_W), lambda i: (i, Z_SGU_BLOCK)),
                  pl.BlockSpec((1, 1, BRANCH_W), lambda i: (layer, 0, 0)),
                  pl.BlockSpec((1, SGU_HEADS, CHUNK, CHUNK), lambda i: (layer, 0, 0, 0)),
                  pl.BlockSpec((1, CHUNK, BRANCH_W), lambda i: (layer, 0, 0))],
        out_specs=pl.BlockSpec((TM, BRANCH_W), lambda i: (i, 0)),
        out_shape=jax.ShapeDtypeStruct((nt, BRANCH_W), MXU_DTYPE),
        compiler_params=_params(("parallel",), vmem),
        name="sgu",
    )(z, g, w_s, bias)


def _conv_kernel(x_ref, b_ref, c_ref, w_ref, o_ref, *, n_ctx_tiles, ctx_period, lat_period):
    t = c_ref[...] * x_ref[...]
    tm = t.shape[0]
    period = jnp.where(pl.program_id(0) < n_ctx_tiles, ctx_period, lat_period)
    pos = lax.broadcasted_iota(jnp.int32, t.shape, 0) & (period - 1)
    prev = jnp.where(pos == 0, 0.0, pltpu.roll(t, 1, axis=0))
    nxt = jnp.where(pos == period - 1, 0.0, pltpu.roll(t, tm - 1, axis=0))
    w = w_ref[0]
    o_ref[...] = (b_ref[...] * (prev * w[0:1] + t * w[1:2] + nxt * w[2:3])).astype(o_ref.dtype)


def _conv(z, w_taps, layer, n_ctx, ctx_period):
    nt = z.shape[0]
    col = lambda k: pl.BlockSpec((TM, BRANCH_W), lambda i: (i, Z_CONV_BLOCK + k))
    vmem = 2 * (3 * _nbytes((TM, BRANCH_W), F32) + _nbytes((TM, BRANCH_W), MXU_DTYPE)) + 4 * _nbytes((TM, BRANCH_W), F32)
    return pl.pallas_call(
        functools.partial(_conv_kernel, n_ctx_tiles=n_ctx // TM, ctx_period=ctx_period, lat_period=GRID_W),
        grid=(nt // TM,),
        in_specs=[col(0), col(1), col(2),
                  pl.BlockSpec((1,) + w_taps.shape[1:], lambda i: (layer, 0, 0))],
        out_specs=pl.BlockSpec((TM, BRANCH_W), lambda i: (i, 0)),
        out_shape=jax.ShapeDtypeStruct((nt, BRANCH_W), MXU_DTYPE),
        compiler_params=_params(("parallel",), vmem),
        name="short_conv",
    )(z, z, z, w_taps)


def _merge_kernel(b0_ref, b1_ref, b2_ref, b3_ref, gate_ref, wup_ref, o_ref):
    acc = None
    for n, b_ref in enumerate((b0_ref, b1_ref, b2_ref, b3_ref)):
        up = jnp.dot(b_ref[...], wup_ref[0, n], preferred_element_type=F32)
        term = gate_ref[:, n * D_MODEL:(n + 1) * D_MODEL].astype(F32) * up
        acc = term if acc is None else acc + term
    o_ref[...] = acc.astype(o_ref.dtype)


def _merge(branches, gates, w_up, layer):
    nt = gates.shape[0]
    vmem = (2 * (4 * _nbytes((TM, BRANCH_W), MXU_DTYPE) + _nbytes((TM, N_BRANCH * D_MODEL), MXU_DTYPE)
                 + _nbytes(w_up.shape[1:], MXU_DTYPE) + _nbytes((TM, D_MODEL), MXU_DTYPE))
            + 2 * _nbytes((TM, D_MODEL), F32))
    br = pl.BlockSpec((TM, BRANCH_W), lambda i: (i, 0))
    return pl.pallas_call(
        _merge_kernel,
        grid=(nt // TM,),
        in_specs=[br, br, br, br,
                  pl.BlockSpec((TM, N_BRANCH * D_MODEL), lambda i: (i, 0)),
                  pl.BlockSpec((1,) + w_up.shape[1:], lambda i: (layer, 0, 0, 0))],
        out_specs=pl.BlockSpec((TM, D_MODEL), lambda i: (i, 0)),
        out_shape=jax.ShapeDtypeStruct((nt, D_MODEL), MXU_DTYPE),
        compiler_params=_params(("parallel",), vmem),
        name="branch_merge",
    )(*branches, gates, w_up)


def _swiglu_partial(h, w1, w3, w2):
    a = jnp.dot(h, w1, preferred_element_type=F32)
    b = jnp.dot(h, w3, preferred_element_type=F32)
    act = (a * jax.nn.sigmoid(a) * b).astype(MXU_DTYPE)
    return jnp.dot(act, w2, preferred_element_type=F32)


def _ffn_kernel(h_ref, w1_ref, w3_ref, w2_ref, x_ref, g_ref, o_ref, acc_ref):
    f = pl.program_id(1)

    @pl.when(f == 0)
    def _():
        acc_ref[...] = jnp.zeros_like(acc_ref)

    acc_ref[...] += _swiglu_partial(h_ref[...], w1_ref[0], w3_ref[0], w2_ref[0])

    @pl.when(f == pl.num_programs(1) - 1)
    def _():
        o_ref[...] = x_ref[...] + g_ref[0, 0] * acc_ref[...]


def _ffn(h, w1, w3, w2, j, x, mods, layer, k_gate, n_ctx, lat_len):
    nt, d = h.shape
    ff = w1.shape[2]
    tm = TM_MM
    once = pl.Buffered(1)
    vmem = (2 * (_nbytes((tm, d), MXU_DTYPE) + 3 * _nbytes((d, TF), MXU_DTYPE)) + 3 * _nbytes((tm, d), F32)
            + 3 * _nbytes((tm, TF), F32))
    return pl.pallas_call(
        _ffn_kernel,
        grid=(nt // tm, ff // TF),
        in_specs=[pl.BlockSpec((tm, d), lambda i, f: (i, 0)),
                  pl.BlockSpec((1, d, TF), lambda i, f: (j, 0, f)),
                  pl.BlockSpec((1, d, TF), lambda i, f: (j, 0, f)),
                  pl.BlockSpec((1, TF, d), lambda i, f: (j, f, 0)),
                  pl.BlockSpec((tm, d), lambda i, f: (i, 0), pipeline_mode=once),
                  _mod_spec(layer, k_gate, tm, n_ctx, lat_len)],
        out_specs=pl.BlockSpec((tm, d), lambda i, f: (i, 0), pipeline_mode=once),
        out_shape=jax.ShapeDtypeStruct((nt, d), F32),
        scratch_shapes=[pltpu.VMEM((tm, d), F32)],
        compiler_params=_params(("parallel", "arbitrary"), vmem),
        name="ffn_dense",
    )(h, w1, w3, w2, x, mods)


def _dispatch_kernel(dest_ref, hp_ref, init_ref, xs_ref, sem):
    del init_ref
    n = hp_ref.shape[0]

    def row_copy(r, dst_row):
        return pltpu.make_async_copy(hp_ref.at[r], xs_ref.at[dst_row], sem)

    def start(r, carry):
        row_copy(r, dest_ref[0, r]).start()
        row_copy(r, dest_ref[1, r]).start()
        return carry

    def wait(r, carry):
        row_copy(r, 0).wait()
        row_copy(r, 0).wait()
        return carry

    lax.fori_loop(0, n, start, 0, unroll=8)
    lax.fori_loop(0, n, wait, 0, unroll=8)


def _dispatch(hp, dest, n_rows):
    nt = hp.shape[0]
    tile = hp.shape[1:]
    vmem = 2 * _nbytes((MOVE_ROWS,) + tile, hp.dtype)
    return pl.pallas_call(
        _dispatch_kernel,
        grid=(nt // MOVE_ROWS,),
        in_specs=[pl.BlockSpec((2, MOVE_ROWS), lambda i: (0, i), memory_space=pltpu.SMEM),
                  pl.BlockSpec((MOVE_ROWS,) + tile, lambda i: (i, 0, 0)),
                  pl.BlockSpec(memory_space=pl.ANY)],
        out_specs=pl.BlockSpec(memory_space=pl.ANY),
        out_shape=jax.ShapeDtypeStruct((n_rows,) + tile, hp.dtype),
        scratch_shapes=[pltpu.SemaphoreType.DMA(())],
        input_output_aliases={2: 0},
        compiler_params=_params(("arbitrary",), vmem),
        name="moe_dispatch",
    )(dest, hp, jnp.zeros((n_rows,) + tile, hp.dtype))


def _moe_ffn_kernel(te_ref, nv_ref, hp_ref, w1_ref, w3_ref, w2_ref, o_ref, h_ref, acc_ref):
    i = pl.program_id(0)
    f = pl.program_id(1)
    last = pl.num_programs(1) - 1

    @pl.when(i < nv_ref[0])
    def _():
        @pl.when(f == 0)
        def _():
            half = h_ref.shape[1] // 2
            lo, hi = _unpack_bf16_pair(_tiles_to_blocks(hp_ref[...]))
            for q in range(ROW_TILES):
                h_ref[:, q * LANES:(q + 1) * LANES] = lo[q].astype(h_ref.dtype)
                h_ref[:, half + q * LANES:half + (q + 1) * LANES] = hi[q].astype(h_ref.dtype)
            acc_ref[...] = jnp.zeros_like(acc_ref)

        acc_ref[...] += _swiglu_partial(h_ref[...], w1_ref[0, 0], w3_ref[0, 0], w2_ref[0, 0])

        @pl.when(f == last)
        def _():
            o_ref[...] = _rows_to_tiles(acc_ref[...])

    @pl.when(jnp.logical_and(i >= nv_ref[0], f == last))
    def _():
        o_ref[...] = jnp.zeros_like(o_ref)


def _moe_ffn(xs, w1, w3, w2, j, tile_expert, n_valid):
    r = xs.shape[0]
    d = D_MODEL
    ff = w1.shape[3]
    tm, tf = MOE_TM, MOE_TF
    vmem = (2 * (_nbytes((tm, d // 2), U32) + 3 * _nbytes((d, tf), MXU_DTYPE) + _nbytes((tm, d), F32))
            + _nbytes((tm, d), MXU_DTYPE) + _nbytes((tm, d), F32) + 3 * _nbytes((tm, tf), F32))

    def live(i, f, nv):
        return jnp.where(i < nv[0], f, 0)

    grid_spec = pltpu.PrefetchScalarGridSpec(
        num_scalar_prefetch=2,
        grid=(r // tm, ff // tf),
        in_specs=[pl.BlockSpec((tm, ROW_TILES, LANES), lambda i, f, te, nv: (i, 0, 0)),
                  pl.BlockSpec((1, 1, d, tf), lambda i, f, te, nv: (j, te[i], 0, live(i, f, nv))),
                  pl.BlockSpec((1, 1, d, tf), lambda i, f, te, nv: (j, te[i], 0, live(i, f, nv))),
                  pl.BlockSpec((1, 1, tf, d), lambda i, f, te, nv: (j, te[i], live(i, f, nv), 0))],
        out_specs=pl.BlockSpec((tm, OUT_TILES, LANES), lambda i, f, te, nv: (i, 0, 0)),
        scratch_shapes=[pltpu.VMEM((tm, d), MXU_DTYPE), pltpu.VMEM((tm, d), F32)],
    )
    return pl.pallas_call(
        _moe_ffn_kernel,
        grid_spec=grid_spec,
        out_shape=jax.ShapeDtypeStruct((r, OUT_TILES, LANES), F32),
        compiler_params=_params(("arbitrary", "arbitrary"), vmem),
        name="moe_ffn",
    )(tile_expert, n_valid, xs, w1, w3, w2)


def _combine_kernel(dest_ref, ys_ref, p_ref, x_ref, g_ref, o_ref, buf_ref, sem):
    n = o_ref.shape[0]

    def row_copy(k, r, src_row):
        return pltpu.make_async_copy(ys_ref.at[src_row], buf_ref.at[k, r], sem)

    def start(r, carry):
        row_copy(0, r, dest_ref[0, r]).start()
        row_copy(1, r, dest_ref[1, r]).start()
        return carry

    def wait(r, carry):
        row_copy(0, r, 0).wait()
        row_copy(1, r, 0).wait()
        return carry

    lax.fori_loop(0, n, start, 0, unroll=8)
    lax.fori_loop(0, n, wait, 0, unroll=8)
    p1 = jnp.broadcast_to(p_ref[:, 0:1], (n, LANES))
    p2 = jnp.broadcast_to(p_ref[:, 1:2], (n, LANES))
    first, second = _tiles_to_blocks(buf_ref[0]), _tiles_to_blocks(buf_ref[1])
    for q in range(OUT_TILES):
        cols = slice(q * LANES, (q + 1) * LANES)
        o_ref[:, cols] = x_ref[:, cols] + g_ref[0, 0][:, cols] * (p1 * first[q] + p2 * second[q])


def _combine(ys, dest, probs, x, mods, layer, k_gate, n_ctx, lat_len):
    nt, d = x.shape
    tc = MOVE_ROWS
    vmem = 2 * (2 * _nbytes((tc, d), F32) + _nbytes((tc, LANES), F32)) + _nbytes((2, tc, d), F32)
    return pl.pallas_call(
        _combine_kernel,
        grid=(nt // tc,),
        in_specs=[pl.BlockSpec((2, tc), lambda i: (0, i), memory_space=pltpu.SMEM),
                  pl.BlockSpec(memory_space=pl.ANY),
                  pl.BlockSpec((tc, LANES), lambda i: (i, 0)),
                  pl.BlockSpec((tc, d), lambda i: (i, 0)),
                  _mod_spec(layer, k_gate, tc, n_ctx, lat_len)],
        out_specs=pl.BlockSpec((tc, d), lambda i: (i, 0)),
        out_shape=jax.ShapeDtypeStruct((nt, d), F32),
        scratch_shapes=[pltpu.VMEM((2, tc, OUT_TILES, LANES), F32), pltpu.SemaphoreType.DMA(())],
        compiler_params=_params(("arbitrary",), vmem),
        name="moe_combine",
    )(dest, ys, probs, x, mods)


def _route(idx, n_rows):
    nt = idx.shape[1]
    e_flat = idx.reshape(-1)
    onehot = (e_flat[:, None] == jnp.arange(N_EXPERTS)[None, :]).astype(jnp.int32)
    counts = jnp.sum(onehot, axis=0)
    rank = jnp.sum((jnp.cumsum(onehot, axis=0) - onehot) * onehot, axis=1)
    padded = ((counts + MOE_TM - 1) // MOE_TM) * MOE_TM
    ends = jnp.cumsum(padded)
    starts = ends - padded
    dest = jnp.sum(onehot * starts[None, :], axis=1) + rank
    tile_start = jnp.arange(n_rows // MOE_TM, dtype=jnp.int32) * MOE_TM
    tile_expert = jnp.minimum(jnp.sum((tile_start[:, None] >= ends[None, :]).astype(jnp.int32), axis=1),
                              N_EXPERTS - 1).astype(jnp.int32)
    n_valid = (ends[-1] // MOE_TM).astype(jnp.int32).reshape(1)
    return dest.reshape(2, nt).astype(jnp.int32), tile_expert, n_valid


def _moe(x, norm_g, mods, layer, w_r_t, b_r, w1, w3, w2, j, n_ctx, lat_len):
    nt = x.shape[0]
    hp, idx, probs = _norm_router(x, norm_g, mods, layer, 4, 3, w_r_t, b_r, j, n_ctx, lat_len)
    n_rows = 2 * nt + N_EXPERTS * MOE_TM
    dest, tile_expert, n_valid = _route(idx, n_rows)
    xs = _dispatch(hp, dest, n_rows)
    ys = _moe_ffn(xs, w1, w3, w2, j, tile_expert, n_valid)
    return _combine(ys, dest, probs, x, mods, layer, 5, n_ctx, lat_len)


def _mixer(x, mods, layer, wts, dims):
    n_ctx, ctx_seqs, ctx_len, lat_seqs, lat_len = dims
    nt = x.shape[0]
    h = _norm_mod(x, wts["norm1"], mods, layer, 1, 0, n_ctx, lat_len)
    z = _mm(h, wts["w_in_rest"], layer, F32, 3 * BRANCH_W, name="in_proj")
    gates = _mm(h, wts["w_gate"], layer, MXU_DTYPE, 2 * BRANCH_W, act="sigmoid", name="gate_proj")

    u_t = _mm_nt(wts["w_in_ssm_t"], h, layer).reshape(SSM_GROUPS, SSM_GROUP, nt // CHUNK, CHUNK)
    y_t, fin = _ssm(u_t, wts["ssm"], layer, ctx_seqs, ctx_len // CHUNK, lat_seqs, lat_len // CHUNK)
    br_ssm = _glu(y_t.reshape(BRANCH_W, nt), wts["w_glu_t"], layer)

    br_fft = jnp.concatenate([_fft(z, 0, ctx_seqs, ctx_len), _fft(z, n_ctx, lat_seqs, lat_len)], axis=0)
    br_sgu = _sgu(z, wts["sgu_norm"], wts["w_s"], wts["sgu_bias"], layer)
    br_conv = _conv(z, wts["conv_taps"], layer, n_ctx, ctx_len)

    merged = _merge((br_ssm, br_fft, br_sgu, br_conv), gates, wts["w_up"], layer)
    x = _mm_residual(merged, wts["w_o"], layer, x, mods, 2, n_ctx, lat_len)
    return x, fin


def kernel(x_prompt, x_sample, c, state_ssm_re, state_ssm_im, c_ctx, norm1_g, norm2_g, w_ada, b_ada, w_in, w_gate,
           ssm_lam_re, ssm_lam_im, ssm_b_re, ssm_b_im, ssm_c_re, ssm_c_im, ssm_log_dt, ssm_d, ssm_w_glu,
           sgu_norm_g, sgu_w_spatial, sgu_b_spatial, conv_w, w_up, w_o, ffn_w1, ffn_w3, ffn_w2,
           moe_w_router, moe_b_router, moe_w1, moe_w3, moe_w2, final_norm_g):
    ctx_seqs, ctx_len, d = x_prompt.shape
    lat_seqs, lat_len, _ = x_sample.shape
    depth = w_in.shape[0]
    n_ctx = ctx_seqs * ctx_len
    dims = (n_ctx, ctx_seqs, ctx_len, lat_seqs, lat_len)
    assert n_ctx % lat_len == 0 and lat_len % TM_MM == 0 and n_ctx % TM_MM == 0 and 1 + lat_seqs <= COND_ROWS
    assert ctx_len % CHUNK == 0 and ctx_len & (ctx_len - 1) == 0 and lat_len % GRID_W == 0 and TM % ctx_len == 0

    x = jnp.concatenate([x_prompt.reshape(n_ctx, d), x_sample.reshape(lat_seqs * lat_len, d)], axis=0)
    cond = jnp.concatenate([c_ctx[None, :], c, jnp.zeros((COND_ROWS - 1 - lat_seqs, d), F32)], axis=0)
    mods = _ada(cond, w_ada, b_ada).reshape(depth, COND_ROWS * N_MOD, 1, d)

    cast = lambda w: w.astype(MXU_DTYPE)
    b1, b2, b4 = BRANCH_W, 2 * BRANCH_W, 4 * BRANCH_W
    head_w = BRANCH_W // SGU_HEADS
    wts = {
        "norm1": norm1_g.reshape(depth, 1, d),
        "w_in_rest": cast(jnp.concatenate([w_in[:, :, b2:b4], w_in[:, :, b1:b2], w_in[:, :, b4:]], axis=2)),
        "w_in_ssm_t": cast(jnp.swapaxes(w_in[:, :, :b1], 1, 2)),
        "w_gate": cast(w_gate),
        "ssm": _ssm_tables(ssm_lam_re, ssm_lam_im, ssm_b_re, ssm_b_im, ssm_c_re, ssm_c_im, ssm_log_dt, ssm_d,
                           state_ssm_re, state_ssm_im),
        "w_glu_t": cast(jnp.swapaxes(ssm_w_glu, 1, 2)),
        "sgu_norm": sgu_norm_g.reshape(depth, 1, BRANCH_W),
        "w_s": cast(sgu_w_spatial),
        "sgu_bias": jnp.repeat(jnp.swapaxes(sgu_b_spatial, 1, 2), head_w, axis=2),
        "conv_taps": jnp.pad(jnp.swapaxes(conv_w, 1, 2), ((0, 0), (0, 5), (0, 0))),
        "w_up": cast(w_up), "w_o": cast(w_o),
    }
    norm2 = norm2_g.reshape(depth, 1, d)
    pad_ff = D_FF_PAD - D_FF
    ffn_w = (cast(jnp.pad(ffn_w1, ((0, 0), (0, 0), (0, pad_ff)))),
             cast(jnp.pad(ffn_w3, ((0, 0), (0, 0), (0, pad_ff)))),
             cast(jnp.pad(ffn_w2, ((0, 0), (0, pad_ff), (0, 0)))))
    moe_w = (cast(moe_w1), cast(moe_w3), cast(moe_w2))
    w_r_t = jnp.swapaxes(moe_w_router, 1, 2)
    b_r = moe_b_router[:, :, None]

    new_re, new_im = [], []
    p = SSM_STATE
    for i in range(depth):
        x, fin = _mixer(x, mods, i, wts, dims)
        fin = fin.transpose(1, 0, 2)
        new_re.append(jnp.stack([fin[..., :p], fin[..., 2 * p:3 * p]], axis=1))
        new_im.append(jnp.stack([fin[..., p:2 * p], fin[..., 3 * p:]], axis=1))
        j = i // 2
        if i % 2 == 0:
            h2 = _norm_mod(x, norm2, mods, i, 4, 3, n_ctx, lat_len)
            x = _ffn(h2, *ffn_w, j, x, mods, i, 5, n_ctx, lat_len)
        else:
            x = _moe(x, norm2, mods, i, w_r_t, b_r, *moe_w, j, n_ctx, lat_len)

    y = _final_norm(x, final_norm_g)
    y_prompt = y[:n_ctx].reshape(ctx_seqs, ctx_len, d)
    y_sample = y[n_ctx:].reshape(lat_seqs, lat_len, d)
    return y_prompt, y_sample, jnp.stack(new_re, axis=1), jnp.stack(new_im, axis=1)
```

```python
import functools
import math

import jax
import jax.numpy as jnp
from jax import lax
from jax.experimental import pallas as pl
from jax.experimental.pallas import tpu as pltpu

F32 = jnp.float32
BF16 = jnp.bfloat16
U32 = jnp.uint32
MXU_DTYPE = BF16

D_MODEL = 2048
BRANCH_W = 512
N_BRANCH = 4
SSM_GROUP = 16
SSM_GROUPS = 32
SSM_STATE = 64
CHUNK = 128
SSM_CENTER = CHUNK // 2
SGU_HEADS = 4
GRID_W = 64
D_FF = 5504
D_FF_PAD = 5632
N_EXPERTS = 8
N_MOD = 6
EPS = 1e-6
COND_ROWS = 16
LANES = 128
ROW_TILES = D_MODEL // 2 // LANES
OUT_TILES = D_MODEL // LANES

TM = 512
TM_MM = 1024
TF = 512
MOE_TM = 512
MOE_TF = 1024
MOVE_ROWS = 256
VMEM_CAP = 60 * 1024 * 1024


def _params(sem, vmem_bytes):
    limit = int(min(VMEM_CAP, max(vmem_bytes + (12 << 20), 32 << 20)))
    return pltpu.CompilerParams(dimension_semantics=sem, vmem_limit_bytes=limit)


def _nbytes(shape, dtype):
    return math.prod(shape) * jnp.dtype(dtype).itemsize


def _seq_row(i, tm, n_ctx, lat_len):
    start = i * tm
    return jnp.where(start < n_ctx, 0, 1 + (start - n_ctx) // lat_len)


def _mod_spec(layer, k, tm, n_ctx, lat_len, grid_rank=1):
    def index(i, *_):
        return (layer, _seq_row(i, tm, n_ctx, lat_len) * N_MOD + k, 0, 0)
    return pl.BlockSpec((1, 1, 1, D_MODEL), index)


def _swap_halves(x):
    return pltpu.roll(x, SSM_STATE, axis=1)


def _ada_kernel(c_ref, w_ref, b_ref, o_ref):
    c = c_ref[...]
    s = (c * jax.nn.sigmoid(c)).astype(MXU_DTYPE)
    o_ref[0] = jnp.dot(s, w_ref[0].astype(MXU_DTYPE), preferred_element_type=F32) + b_ref[0]


def _ada(cond, w_ada, b_ada):
    depth, d, n = w_ada.shape
    tn = 1024
    vmem = 2 * (_nbytes((COND_ROWS, d), F32) + _nbytes((d, tn), F32) + _nbytes((COND_ROWS, tn), F32))
    return pl.pallas_call(
        _ada_kernel,
        grid=(depth, n // tn),
        in_specs=[pl.BlockSpec((COND_ROWS, d), lambda l, j: (0, 0)),
                  pl.BlockSpec((1, d, tn), lambda l, j: (l, 0, j)),
                  pl.BlockSpec((1, 1, tn), lambda l, j: (l, 0, j))],
        out_specs=pl.BlockSpec((1, COND_ROWS, tn), lambda l, j: (l, 0, j)),
        out_shape=jax.ShapeDtypeStruct((depth, COND_ROWS, n), F32),
        compiler_params=_params(("parallel", "parallel"), vmem),
        name="ada",
    )(cond, w_ada, b_ada.reshape(depth, 1, n))


def _rmsnorm(x, g):
    return x * lax.rsqrt(jnp.mean(x * x, axis=-1, keepdims=True) + EPS) * g


def _norm_mod_kernel(x_ref, g_ref, sc_ref, sh_ref, o_ref):
    y = _rmsnorm(x_ref[...], g_ref[0]) * (1.0 + sc_ref[0, 0]) + sh_ref[0, 0]
    o_ref[...] = y.astype(o_ref.dtype)


def _norm_kernel(x_ref, g_ref, o_ref):
    o_ref[...] = _rmsnorm(x_ref[...], g_ref[...]).astype(o_ref.dtype)


def _pack_bf16_pair(lo, hi):
    lo_b = pltpu.bitcast(lo.astype(BF16).astype(F32), U32)
    hi_b = pltpu.bitcast(hi.astype(BF16).astype(F32), U32)
    return (lo_b >> 16) | (hi_b & jnp.uint32(0xFFFF0000))


def _unpack_bf16_pair(p):
    lo = pltpu.bitcast(p << 16, F32)
    hi = pltpu.bitcast(p & jnp.uint32(0xFFFF0000), F32)
    return lo, hi


def _rows_to_tiles(x):
    blocks = jnp.stack([x[:, q * LANES:(q + 1) * LANES] for q in range(x.shape[1] // LANES)], axis=0)
    return pltpu.einshape("qrl->rql", blocks)


def _tiles_to_blocks(t):
    return pltpu.einshape("rql->qrl", t)


def _norm_router_kernel(x_ref, g_ref, sc_ref, sh_ref, wr_ref, br_ref, hp_ref, idx_ref, prob_ref):
    h = _rmsnorm(x_ref[...], g_ref[0]) * (1.0 + sc_ref[0, 0]) + sh_ref[0, 0]
    half = h.shape[1] // 2
    hp_ref[...] = _rows_to_tiles(_pack_bf16_pair(h[:, :half], h[:, half:]))
    logits = lax.dot_general(wr_ref[0], h, (((1,), (1,)), ((), ())),
                             precision=lax.Precision.HIGHEST,
                             preferred_element_type=F32) + br_ref[0]
    e_iota = lax.broadcasted_iota(jnp.int32, logits.shape, 0).astype(F32)
    none = float(N_EXPERTS)
    m1 = jnp.max(logits, axis=0, keepdims=True)
    i1 = jnp.min(jnp.where(logits == m1, e_iota, none), axis=0, keepdims=True)
    rest = jnp.where(e_iota == i1, -jnp.inf, logits)
    m2 = jnp.max(rest, axis=0, keepdims=True)
    i2 = jnp.min(jnp.where(rest == m2, e_iota, none), axis=0, keepdims=True)
    r = jnp.exp(m2 - m1)
    p1 = 1.0 / (1.0 + r)
    idx_ref[...] = jnp.concatenate([i1, i2], axis=0).astype(jnp.int32)
    rows = jnp.concatenate([p1, r * p1, jnp.zeros((LANES - 2, p1.shape[1]), F32)], axis=0)
    prob_ref[...] = rows.T


def _norm_mod(x, g, mods, layer, k_scale, k_shift, n_ctx, lat_len):
    nt, d = x.shape
    vmem = 2 * (_nbytes((TM, d), F32) + _nbytes((TM, d), MXU_DTYPE))
    return pl.pallas_call(
        _norm_mod_kernel,
        grid=(nt // TM,),
        in_specs=[pl.BlockSpec((TM, d), lambda i: (i, 0)),
                  pl.BlockSpec((1, 1, d), lambda i: (layer, 0, 0)),
                  _mod_spec(layer, k_scale, TM, n_ctx, lat_len),
                  _mod_spec(layer, k_shift, TM, n_ctx, lat_len)],
        out_specs=pl.BlockSpec((TM, d), lambda i: (i, 0)),
        out_shape=jax.ShapeDtypeStruct((nt, d), MXU_DTYPE),
        compiler_params=_params(("parallel",), vmem),
        name="norm_mod",
    )(x, g, mods, mods)


def _final_norm(x, g, row0, rows):
    d = x.shape[1]
    tile0 = row0 // TM
    vmem = 4 * _nbytes((TM, d), F32)
    return pl.pallas_call(
        _norm_kernel,
        grid=(rows // TM,),
        in_specs=[pl.BlockSpec((TM, d), lambda i: (tile0 + i, 0)),
                  pl.BlockSpec((1, d), lambda i: (0, 0))],
        out_specs=pl.BlockSpec((TM, d), lambda i: (i, 0)),
        out_shape=jax.ShapeDtypeStruct((rows, d), F32),
        compiler_params=_params(("parallel",), vmem),
        name="final_norm",
    )(x, g.reshape(1, d))


def _norm_router(x, g, mods, layer, k_scale, k_shift, w_r_t, b_r, j, n_ctx, lat_len):
    nt, d = x.shape
    vmem = 2 * (_nbytes((TM, d), F32) + _nbytes((TM, d // 2), U32) + _nbytes((TM, LANES), F32))
    return pl.pallas_call(
        _norm_router_kernel,
        grid=(nt // TM,),
        in_specs=[pl.BlockSpec((TM, d), lambda i: (i, 0)),
                  pl.BlockSpec((1, 1, d), lambda i: (layer, 0, 0)),
                  _mod_spec(layer, k_scale, TM, n_ctx, lat_len),
                  _mod_spec(layer, k_shift, TM, n_ctx, lat_len),
                  pl.BlockSpec((1, N_EXPERTS, d), lambda i: (j, 0, 0)),
                  pl.BlockSpec((1, N_EXPERTS, 1), lambda i: (j, 0, 0))],
        out_specs=[pl.BlockSpec((TM, ROW_TILES, LANES), lambda i: (i, 0, 0)),
                   pl.BlockSpec((2, TM), lambda i: (0, i)),
                   pl.BlockSpec((TM, LANES), lambda i: (i, 0))],
        out_shape=[jax.ShapeDtypeStruct((nt, ROW_TILES, LANES), U32),
                   jax.ShapeDtypeStruct((2, nt), jnp.int32),
                   jax.ShapeDtypeStruct((nt, LANES), F32)],
        compiler_params=_params(("parallel",), vmem),
        name="norm_router",
    )(x, g, mods, mods, w_r_t, b_r)


def _mm_kernel(x_ref, w_ref, o_ref, *, act):
    y = jnp.dot(x_ref[...], w_ref[0], preferred_element_type=F32)
    if act == "sigmoid":
        y = jax.nn.sigmoid(y)
    o_ref[...] = y.astype(o_ref.dtype)


def _mm(x, w, layer, out_dtype, tn, act=None, name="mm"):
    m, k = x.shape
    n = w.shape[2]
    vmem = 2 * (_nbytes((TM_MM, k), x.dtype) + _nbytes((k, tn), w.dtype) + _nbytes((TM_MM, tn), out_dtype))
    return pl.pallas_call(
        functools.partial(_mm_kernel, act=act),
        grid=(m // TM_MM, n // tn),
        in_specs=[pl.BlockSpec((TM_MM, k), lambda i, j: (i, 0)),
                  pl.BlockSpec((1, k, tn), lambda i, j: (layer, 0, j))],
        out_specs=pl.BlockSpec((TM_MM, tn), lambda i, j: (i, j)),
        out_shape=jax.ShapeDtypeStruct((m, n), out_dtype),
        compiler_params=_params(("parallel", "parallel"), vmem),
        name=name,
    )(x, w)


def _gate_kernel(x_ref, w_ref, s1_ref, s3_ref, s2_ref, o_ref, c1_ref, c3_ref, c2_ref):
    o_ref[...] = jax.nn.sigmoid(jnp.dot(x_ref[...], w_ref[0], preferred_element_type=F32)).astype(o_ref.dtype)
    c1_ref[...] = s1_ref[0].astype(c1_ref.dtype)
    c3_ref[...] = s3_ref[0].astype(c3_ref.dtype)
    c2_ref[...] = s2_ref[0].astype(c2_ref.dtype)


def _gate_proj_with_cast(h, w_gate, layer, moe_f32):
    m, k = h.shape
    n = w_gate.shape[2]
    tn = BRANCH_W
    n_i, n_j = m // TM_MM, n // tn
    steps = n_i * n_j
    jl = layer // 2
    flat = [w.reshape(w.shape[0], -1, w.shape[-1]) for w in moe_f32]
    n_slabs = 1 << (steps.bit_length() - 1)
    bf16_rows = 16
    assert all(w.shape[1] % n_slabs == 0 and (w.shape[1] // n_slabs) % bf16_rows == 0 for w in flat)
    slab_rows = [w.shape[1] // n_slabs for w in flat]

    def slab(i, j):
        return jnp.minimum(i * n_j + j, n_slabs - 1)

    side_in = [pl.BlockSpec((1, r, w.shape[2]), lambda i, j: (jl, slab(i, j), 0)) for r, w in zip(slab_rows, flat)]
    side_out = [pl.BlockSpec((r, w.shape[2]), lambda i, j: (slab(i, j), 0)) for r, w in zip(slab_rows, flat)]
    cast_shapes = [jax.ShapeDtypeStruct(w.shape[1:], MXU_DTYPE) for w in flat]
    side_bytes = sum(_nbytes((r, w.shape[2]), F32) + _nbytes((r, w.shape[2]), MXU_DTYPE)
                     for r, w in zip(slab_rows, flat))
    vmem = 2 * (_nbytes((TM_MM, k), h.dtype) + _nbytes((k, tn), w_gate.dtype) + _nbytes((TM_MM, tn), MXU_DTYPE)
                + side_bytes) + _nbytes((TM_MM, tn), F32)
    out = pl.pallas_call(
        _gate_kernel,
        grid=(n_i, n_j),
        in_specs=[pl.BlockSpec((TM_MM, k), lambda i, j: (i, 0)),
                  pl.BlockSpec((1, k, tn), lambda i, j: (layer, 0, j))] + side_in,
        out_specs=[pl.BlockSpec((TM_MM, tn), lambda i, j: (i, j))] + side_out,
        out_shape=[jax.ShapeDtypeStruct((m, n), MXU_DTYPE)] + cast_shapes,
        compiler_params=_params(("arbitrary", "arbitrary"), vmem),
        name="gate_proj_cast",
    )(h, w_gate, *flat)
    return out[0], tuple(out[1:])


def _mm_nt_kernel(w_ref, x_ref, o_ref):
    o_ref[...] = lax.dot_general(w_ref[0], x_ref[...], (((1,), (1,)), ((), ())),
                                 preferred_element_type=F32)


def _mm_nt(w_t, x, layer):
    m, k = x.shape
    n = w_t.shape[1]
    vmem = 2 * (_nbytes((TM_MM, k), x.dtype) + _nbytes((n, k), w_t.dtype) + _nbytes((n, TM_MM), F32))
    return pl.pallas_call(
        _mm_nt_kernel,
        grid=(m // TM_MM,),
        in_specs=[pl.BlockSpec((1, n, k), lambda i: (layer, 0, 0)),
                  pl.BlockSpec((TM_MM, k), lambda i: (i, 0))],
        out_specs=pl.BlockSpec((n, TM_MM), lambda i: (0, i)),
        out_shape=jax.ShapeDtypeStruct((n, m), F32),
        compiler_params=_params(("parallel",), vmem),
        name="ssm_proj",
    )(w_t, x)


def _mm_res_kernel(a_ref, w_ref, x_ref, g_ref, o_ref):
    y = jnp.dot(a_ref[...], w_ref[0], preferred_element_type=F32)
    o_ref[...] = x_ref[...] + g_ref[0, 0] * y


def _mm_residual(a, w, layer, x, mods, k_gate, n_ctx, lat_len):
    m, k = a.shape
    n = w.shape[2]
    vmem = 2 * (_nbytes((TM, k), a.dtype) + _nbytes((k, n), w.dtype) + 2 * _nbytes((TM, n), F32))
    return pl.pallas_call(
        _mm_res_kernel,
        grid=(m // TM,),
        in_specs=[pl.BlockSpec((TM, k), lambda i: (i, 0)),
                  pl.BlockSpec((1, k, n), lambda i: (layer, 0, 0)),
                  pl.BlockSpec((TM, n), lambda i: (i, 0)),
                  _mod_spec(layer, k_gate, TM, n_ctx, lat_len)],
        out_specs=pl.BlockSpec((TM, n), lambda i: (i, 0)),
        out_shape=jax.ShapeDtypeStruct((m, n), F32),
        compiler_params=_params(("parallel",), vmem),
        name="out_proj",
    )(a, w, x, mods)


PW_LF, PW_RF, PW_LB, PW_RB, PW_SF, PW_SB, PW_OF, PW_OB = range(8)
V_LF, V_LB, V_RF, V_RB, V_SF, V_SB, V_OF, V_OB = (2 * k for k in range(8))


def _ssm_kernel(u_ref, pw_ref, vec_ref, lam_ref, h0_ref, d_ref, y_ref, fin_ref,
                lf_ref, lb_ref, rf_ref, rb_ref, wst_ref, wout_ref, toep_ref, sf_ref, sb_ref, hf_ref, hb_ref,
                *, ctx_seqs, ctx_chunks, lat_seqs, lat_chunks):
    kdim = SSM_GROUP * CHUNK
    half = 2 * SSM_STATE

    def fill(dst_ref, lanes, table, vset):
        tab = pw_ref[0, 0, table]
        tab_sw = _swap_halves(tab)
        for j in range(SSM_GROUP):
            v1 = vec_ref[0, 0, vset, j:j + 1, :]
            v2 = vec_ref[0, 0, vset + 1, j:j + 1, :]
            dst_ref[j * CHUNK:(j + 1) * CHUNK, lanes] = (tab * v1 + tab_sw * v2).astype(dst_ref.dtype)

    fwd, bwd = slice(0, half), slice(half, 2 * half)
    fill(lf_ref, fwd, PW_LF, V_LF)
    fill(lb_ref, fwd, PW_LB, V_LB)
    fill(rf_ref, fwd, PW_RF, V_RF)
    fill(rb_ref, fwd, PW_RB, V_RB)
    fill(wst_ref, fwd, PW_SF, V_SF)
    fill(wst_ref, bwd, PW_SB, V_SB)
    fill(wout_ref, fwd, PW_OF, V_OF)
    fill(wout_ref, bwd, PW_OB, V_OB)

    row_s = lax.broadcasted_iota(jnp.int32, (CHUNK, kdim), 0)
    col_t = lax.broadcasted_iota(jnp.int32, (CHUNK, kdim), 1) & (CHUNK - 1)
    causal = col_t >= row_s
    anti = col_t <= row_s
    nt_dims = (((1,), (1,)), ((), ()))

    def build(j, carry):
        rows = pl.ds(pl.multiple_of(j * CHUNK, CHUNK), CHUNK)
        a = lax.dot_general(lf_ref[rows, :], rf_ref[...], nt_dims, preferred_element_type=F32)
        b = lax.dot_general(lb_ref[rows, :], rb_ref[...], nt_dims, preferred_element_type=F32)
        toep_ref[rows, :] = (jnp.where(causal, a, 0.0) + jnp.where(anti, b, 0.0)).astype(toep_ref.dtype)
        return carry

    lax.fori_loop(0, SSM_GROUP, build, 0, unroll=4)

    u = jnp.concatenate([u_ref[0, j] for j in range(SSM_GROUP)], axis=-1)
    ub = u.astype(MXU_DTYPE)
    s_all = jnp.dot(ub, wst_ref[...], preferred_element_type=F32)
    sf_ref[...] = s_all[:, fwd]
    sb_ref[...] = s_all[:, bwd]

    def advance(h, s, lanes):
        return h * lam_ref[0, 0, 0:1, lanes] + _swap_halves(h) * lam_ref[0, 0, 1:2, lanes] + s

    def chain(base, seqs, chunks, h_fwd, h_bwd):
        rows = lambda c: pl.ds(base + c, seqs, stride=chunks) if chunks > 1 else pl.ds(base, seqs)
        for c in range(chunks):
            hf_ref[rows(c), :] = h_fwd
            h_fwd = advance(h_fwd, sf_ref[rows(c), :], fwd)
        for c in reversed(range(chunks)):
            hb_ref[rows(c), :] = h_bwd
            h_bwd = advance(h_bwd, sb_ref[rows(c), :], bwd)
        return h_fwd, h_bwd

    zeros = jnp.zeros((ctx_seqs, half), F32)
    end_fwd, end_bwd = chain(0, ctx_seqs, ctx_chunks, zeros, zeros)
    fin_ref[0, :, fwd] = end_fwd
    fin_ref[0, :, bwd] = end_bwd
    chain(ctx_seqs * ctx_chunks, lat_seqs, lat_chunks, h0_ref[0, 0, :, fwd], h0_ref[0, 0, :, bwd])

    hin = jnp.concatenate([hf_ref[...], hb_ref[...]], axis=-1).astype(MXU_DTYPE)
    nblk = 4
    for q in range(SSM_GROUP // nblk):
        cols = slice(q * nblk * CHUNK, (q + 1) * nblk * CHUNK)
        y = (jnp.dot(ub, toep_ref[:, cols], preferred_element_type=F32)
             + lax.dot_general(hin, wout_ref[cols, :], nt_dims, preferred_element_type=F32)
             + u[:, cols] * d_ref[0, 0][:, cols])
        for r in range(nblk):
            y_ref[0, q * nblk + r] = y[:, r * CHUNK:(r + 1) * CHUNK]


def _ssm(u_t, tabs, layer, ctx_seqs, ctx_chunks, lat_seqs, lat_chunks):
    g, hh, rows, _ = u_t.shape
    kdim = SSM_GROUP * CHUNK
    ns = 4 * SSM_STATE
    half = 2 * SSM_STATE
    n_pw, n_vec = tabs["pw"].shape[2], tabs["vec"].shape[2]
    vmem = (2 * (2 * _nbytes((hh, rows, CHUNK), F32) + _nbytes((n_pw, CHUNK, LANES), F32)
                 + _nbytes((n_vec, SSM_GROUP, LANES), F32))
            + 4 * _nbytes((kdim, half), MXU_DTYPE) + 2 * _nbytes((kdim, ns), MXU_DTYPE)
            + _nbytes((kdim, kdim), MXU_DTYPE) + 2 * _nbytes((rows, ns), F32)
            + 3 * _nbytes((rows, kdim), F32))
    per_lg = lambda *shape: pl.BlockSpec((1, 1) + shape, lambda i: (layer, i) + (0,) * len(shape))
    per_g = lambda *shape: pl.BlockSpec((1,) + shape, lambda i: (i,) + (0,) * len(shape))
    return pl.pallas_call(
        functools.partial(_ssm_kernel, ctx_seqs=ctx_seqs, ctx_chunks=ctx_chunks,
                          lat_seqs=lat_seqs, lat_chunks=lat_chunks),
        grid=(g,),
        in_specs=[per_g(hh, rows, CHUNK),
                  per_lg(n_pw, CHUNK, LANES), per_lg(n_vec, SSM_GROUP, LANES),
                  per_lg(2, ns), per_lg(lat_seqs, ns), per_lg(1, kdim)],
        out_specs=[per_g(hh, rows, CHUNK), per_g(ctx_seqs, ns)],
        out_shape=[jax.ShapeDtypeStruct(u_t.shape, F32),
                   jax.ShapeDtypeStruct((g, ctx_seqs, ns), F32)],
        scratch_shapes=[pltpu.VMEM((kdim, half), MXU_DTYPE), pltpu.VMEM((kdim, half), MXU_DTYPE),
                        pltpu.VMEM((kdim, half), MXU_DTYPE), pltpu.VMEM((kdim, half), MXU_DTYPE),
                        pltpu.VMEM((kdim, ns), MXU_DTYPE), pltpu.VMEM((kdim, ns), MXU_DTYPE),
                        pltpu.VMEM((kdim, kdim), MXU_DTYPE),
                        pltpu.VMEM((rows, half), F32), pltpu.VMEM((rows, half), F32),
                        pltpu.VMEM((rows, half), F32), pltpu.VMEM((rows, half), F32)],
        compiler_params=_params(("parallel",), vmem),
        name="s5_chunked",
    )(u_t, tabs["pw"], tabs["vec"], tabs["lam"], tabs["h0"], tabs["d"])


def _cmul(ar, ai, br, bi):
    return ar * br - ai * bi, ar * bi + ai * br


def _ssm_tables(lam_re, lam_im, b_re, b_im, c_re, c_im, log_dt, d_skip, state_re, state_im):
    dt = jnp.exp(log_dt)[..., None]
    ea, eb = lam_re * dt, lam_im * dt
    mag = jnp.exp(ea)
    lbar_re, lbar_im = mag * jnp.cos(eb), mag * jnp.sin(eb)
    den = lam_re * lam_re + lam_im * lam_im
    nr, ni = lbar_re - 1.0, lbar_im
    coef_re = (nr * lam_re + ni * lam_im) / den
    coef_im = (ni * lam_re - nr * lam_im) / den
    bb_re, bb_im = _cmul(coef_re[..., None], coef_im[..., None], b_re, b_im)
    bb_re, bb_im = jnp.swapaxes(bb_re, 3, 4), jnp.swapaxes(bb_im, 3, 4)

    def power(dr, n):
        n = n.astype(F32)[None, None, :, None]
        m = jnp.exp(ea[:, dr, :, None, :] * n)
        ang = eb[:, dr, :, None, :] * n
        return jnp.concatenate([m * jnp.cos(ang), m * jnp.sin(ang)], axis=-1)

    pos = jnp.arange(CHUNK)
    ctr = SSM_CENTER
    pw = jnp.stack([power(0, ctr - pos), power(0, pos - ctr), power(1, pos - ctr), power(1, ctr - pos),
                    power(0, CHUNK - 1 - pos), power(1, pos), power(0, pos + 1), power(1, CHUNK - pos)],
                   axis=2)

    cat = lambda *xs: jnp.concatenate(xs, axis=-1)
    conj_rows = lambda re, im: [cat(re, -re), cat(-im, -im)]
    plain_rows = lambda re, im: [cat(re, re), cat(-im, im)]
    vec = jnp.stack(conj_rows(bb_re[:, 0], bb_im[:, 0]) + conj_rows(bb_re[:, 1], bb_im[:, 1])
                    + plain_rows(c_re[:, 0], c_im[:, 0]) + plain_rows(c_re[:, 1], c_im[:, 1])
                    + plain_rows(bb_re[:, 0], bb_im[:, 0]) + plain_rows(bb_re[:, 1], bb_im[:, 1])
                    + conj_rows(c_re[:, 0], c_im[:, 0]) + conj_rows(c_re[:, 1], c_im[:, 1]),
                   axis=2)

    pt = power(0, jnp.array([CHUNK])), power(1, jnp.array([CHUNK]))
    p = SSM_STATE
    lam = jnp.concatenate(
        [cat(pt[0][..., :p], pt[0][..., :p], pt[1][..., :p], pt[1][..., :p]),
         cat(-pt[0][..., p:], pt[0][..., p:], -pt[1][..., p:], pt[1][..., p:])], axis=2)
    d_rows = jnp.repeat(d_skip.reshape(-1, SSM_GROUPS, 1, SSM_GROUP), CHUNK, axis=-1)
    h0 = cat(state_re[:, :, 0], state_im[:, :, 0], state_re[:, :, 1], state_im[:, :, 1]).transpose(1, 2, 0, 3)
    return {"pw": pw, "vec": vec, "lam": lam, "d": d_rows, "h0": h0}


def _glu_kernel(y_ref, w_ref, o_ref):
    z = jax.nn.gelu(y_ref[...])
    gate = jax.nn.sigmoid(jnp.dot(w_ref[0], z.astype(MXU_DTYPE), preferred_element_type=F32))
    o_ref[...] = (z * gate).T.astype(o_ref.dtype)


def _glu(y_t, w_t, layer):
    w_, nt = y_t.shape
    vmem = 2 * (_nbytes((w_, TM), F32) + _nbytes((w_, w_), MXU_DTYPE) + _nbytes((TM, w_), MXU_DTYPE))
    return pl.pallas_call(
        _glu_kernel,
        grid=(nt // TM,),
        in_specs=[pl.BlockSpec((w_, TM), lambda i: (0, i)),
                  pl.BlockSpec((1, w_, w_), lambda i: (layer, 0, 0))],
        out_specs=pl.BlockSpec((TM, w_), lambda i: (i, 0)),
        out_shape=jax.ShapeDtypeStruct((nt, w_), MXU_DTYPE),
        compiler_params=_params(("parallel",), vmem),
        name="s5_glu",
    )(y_t, w_t)


Z_SGU_BLOCK, Z_FFT_BLOCK, Z_CONV_BLOCK = 0, 2, 3


def _fft_kernel(x_ref, cs_ref, dft_ref, o_ref, pq_ref):
    seq_len = x_ref.shape[0]

    @pl.when(pl.program_id(1) == 0)
    def _():
        pq = jnp.dot(x_ref[...].astype(MXU_DTYPE), cs_ref[...], preferred_element_type=F32)
        pq_ref[:seq_len, :] = pq[:, :BRANCH_W].astype(pq_ref.dtype)
        pq_ref[seq_len:, :] = pq[:, BRANCH_W:].astype(pq_ref.dtype)

    o_ref[...] = jnp.dot(dft_ref[...], pq_ref[...], preferred_element_type=F32).astype(o_ref.dtype)


def _dft_tables(seq_len):
    def cs(n):
        k = jnp.arange(n, dtype=jnp.int32)
        ang = ((k[:, None] * k[None, :]) % n).astype(F32) * (2.0 * math.pi / n)
        return jnp.cos(ang) / math.sqrt(n), jnp.sin(ang) / math.sqrt(n)

    cl, sl = cs(seq_len)
    cc, sc = cs(CHUNK)
    eye = jnp.eye(BRANCH_W // CHUNK, dtype=F32)
    chan = jnp.concatenate([jnp.kron(eye, cc), jnp.kron(eye, sc)], axis=1)
    return jnp.concatenate([cl, -sl], axis=1).astype(MXU_DTYPE), chan.astype(MXU_DTYPE)


def _fft(z, row0, seqs, seq_len):
    dft, chan = _dft_tables(seq_len)
    tm = min(seq_len, TM)
    vmem = (2 * (_nbytes((seq_len, BRANCH_W), F32) + _nbytes(chan.shape, MXU_DTYPE)
                 + _nbytes((tm, 2 * seq_len), MXU_DTYPE) + _nbytes((tm, BRANCH_W), MXU_DTYPE))
            + _nbytes((2 * seq_len, BRANCH_W), MXU_DTYPE) + _nbytes((seq_len, 2 * BRANCH_W), F32))
    seq0 = row0 // seq_len
    return pl.pallas_call(
        _fft_kernel,
        grid=(seqs, seq_len // tm),
        in_specs=[pl.BlockSpec((seq_len, BRANCH_W), lambda b, j: (seq0 + b, Z_FFT_BLOCK)),
                  pl.BlockSpec(chan.shape, lambda b, j: (0, 0)),
                  pl.BlockSpec((tm, 2 * seq_len), lambda b, j: (j, 0))],
        out_specs=pl.BlockSpec((tm, BRANCH_W), lambda b, j: (b * (seq_len // tm) + j, 0)),
        out_shape=jax.ShapeDtypeStruct((seqs * seq_len, BRANCH_W), MXU_DTYPE),
        scratch_shapes=[pltpu.VMEM((2 * seq_len, BRANCH_W), MXU_DTYPE)],
        compiler_params=_params(("parallel", "arbitrary"), vmem),
        name="fnet_dft",
    )(z, chan, dft)


def _sgu_kernel(z_ref, g_ref, ws_ref, bs_ref, o_ref):
    uv = jax.nn.gelu(z_ref[...])
    u = uv[:, :BRANCH_W]
    v = _rmsnorm(uv[:, BRANCH_W:], g_ref[0]).astype(MXU_DTYPE)
    head_w = BRANCH_W // SGU_HEADS
    for c in range(z_ref.shape[0] // CHUNK):
        rows = slice(c * CHUNK, (c + 1) * CHUNK)
        s = jnp.concatenate(
            [jnp.dot(ws_ref[0, h], v[rows, h * head_w:(h + 1) * head_w], preferred_element_type=F32)
             for h in range(SGU_HEADS)], axis=-1)
        o_ref[rows, :] = (u[rows, :] * (s + bs_ref[0])).astype(o_ref.dtype)


def _sgu(z, g, w_s, bias, layer):
    nt = z.shape[0]
    vmem = 2 * (_nbytes((TM, 2 * BRANCH_W), F32) + _nbytes((TM, BRANCH_W), MXU_DTYPE)) + 3 * _nbytes((TM, 2 * BRANCH_W), F32)
    return pl.pallas_call(
        _sgu_kernel,
        grid=(nt // TM,),
        in_specs=[pl.BlockSpec((TM, 2 * BRANCH_W), lambda i: (i, Z_SGU_BLOCK)),
                  pl.BlockSpec((1, 1, BRANCH_W), lambda i: (layer, 0, 0)),
                  pl.BlockSpec((1, SGU_HEADS, CHUNK, CHUNK), lambda i: (layer, 0, 0, 0)),
                  pl.BlockSpec((1, CHUNK, BRANCH_W), lambda i: (layer, 0, 0))],
        out_specs=pl.BlockSpec((TM, BRANCH_W), lambda i: (i, 0)),
        out_shape=jax.ShapeDtypeStruct((nt, BRANCH_W), MXU_DTYPE),
        compiler_params=_params(("parallel",), vmem),
        name="sgu",
    )(z, g, w_s, bias)


def _conv_kernel(x_ref, b_ref, c_ref, w_ref, o_ref, *, n_ctx_tiles, ctx_period, lat_period):
    t = c_ref[...] * x_ref[...]
    tm = t.shape[0]
    period = jnp.where(pl.program_id(0) < n_ctx_tiles, ctx_period, lat_period)
    pos = lax.broadcasted_iota(jnp.int32, t.shape, 0) & (period - 1)
    prev = jnp.where(pos == 0, 0.0, pltpu.roll(t, 1, axis=0))
    nxt = jnp.where(pos == period - 1, 0.0, pltpu.roll(t, tm - 1, axis=0))
    w = w_ref[0]
    o_ref[...] = (b_ref[...] * (prev * w[0:1] + t * w[1:2] + nxt * w[2:3])).astype(o_ref.dtype)


def _conv(z, w_taps, layer, n_ctx, ctx_period):
    nt = z.shape[0]
    col = lambda k: pl.BlockSpec((TM, BRANCH_W), lambda i: (i, Z_CONV_BLOCK + k))
    vmem = 2 * (3 * _nbytes((TM, BRANCH_W), F32) + _nbytes((TM, BRANCH_W), MXU_DTYPE)) + 4 * _nbytes((TM, BRANCH_W), F32)
    return pl.pallas_call(
        functools.partial(_conv_kernel, n_ctx_tiles=n_ctx // TM, ctx_period=ctx_period, lat_period=GRID_W),
        grid=(nt // TM,),
        in_specs=[col(0), col(1), col(2),
                  pl.BlockSpec((1,) + w_taps.shape[1:], lambda i: (layer, 0, 0))],
        out_specs=pl.BlockSpec((TM, BRANCH_W), lambda i: (i, 0)),
        out_shape=jax.ShapeDtypeStruct((nt, BRANCH_W), MXU_DTYPE),
        compiler_params=_params(("parallel",), vmem),
        name="short_conv",
    )(z, z, z, w_taps)


def _merge_kernel(b0_ref, b1_ref, b2_ref, b3_ref, gate_ref, wup_ref, o_ref):
    acc = None
    for n, b_ref in enumerate((b0_ref, b1_ref, b2_ref, b3_ref)):
        up = jnp.dot(b_ref[...], wup_ref[0, n], preferred_element_type=F32)
        term = gate_ref[:, n * D_MODEL:(n + 1) * D_MODEL].astype(F32) * up
        acc = term if acc is None else acc + term
    o_ref[...] = acc.astype(o_ref.dtype)


def _merge(branches, gates, w_up, layer):
    nt = gates.shape[0]
    vmem = (2 * (4 * _nbytes((TM, BRANCH_W), MXU_DTYPE) + _nbytes((TM, N_BRANCH * D_MODEL), MXU_DTYPE)
                 + _nbytes(w_up.shape[1:], MXU_DTYPE) + _nbytes((TM, D_MODEL), MXU_DTYPE))
            + 2 * _nbytes((TM, D_MODEL), F32))
    br = pl.BlockSpec((TM, BRANCH_W), lambda i: (i, 0))
    return pl.pallas_call(
        _merge_kernel,
        grid=(nt // TM,),
        in_specs=[br, br, br, br,
                  pl.BlockSpec((TM, N_BRANCH * D_MODEL), lambda i: (i, 0)),
                  pl.BlockSpec((1,) + w_up.shape[1:], lambda i: (layer, 0, 0, 0))],
        out_specs=pl.BlockSpec((TM, D_MODEL), lambda i: (i, 0)),
        out_shape=jax.ShapeDtypeStruct((nt, D_MODEL), MXU_DTYPE),
        compiler_params=_params(("parallel",), vmem),
        name="branch_merge",
    )(*branches, gates, w_up)


def _swiglu_partial(h, w1, w3, w2):
    a = jnp.dot(h, w1, preferred_element_type=F32)
    b = jnp.dot(h, w3, preferred_element_type=F32)
    act = (a * jax.nn.sigmoid(a) * b).astype(MXU_DTYPE)
    return jnp.dot(act, w2, preferred_element_type=F32)


def _ffn_kernel(h_ref, w1_ref, w3_ref, w2_ref, x_ref, g_ref, o_ref, acc_ref):
    f = pl.program_id(1)

    @pl.when(f == 0)
    def _():
        acc_ref[...] = jnp.zeros_like(acc_ref)

    acc_ref[...] += _swiglu_partial(h_ref[...], w1_ref[0], w3_ref[0], w2_ref[0])

    @pl.when(f == pl.num_programs(1) - 1)
    def _():
        o_ref[...] = x_ref[...] + g_ref[0, 0] * acc_ref[...]


def _ffn(h, w1, w3, w2, j, x, mods, layer, k_gate, n_ctx, lat_len):
    nt, d = h.shape
    ff = w1.shape[2]
    tm = TM_MM
    once = pl.Buffered(1)
    vmem = (2 * (_nbytes((tm, d), MXU_DTYPE) + 3 * _nbytes((d, TF), MXU_DTYPE)) + 3 * _nbytes((tm, d), F32)
            + 3 * _nbytes((tm, TF), F32))
    return pl.pallas_call(
        _ffn_kernel,
        grid=(nt // tm, ff // TF),
        in_specs=[pl.BlockSpec((tm, d), lambda i, f: (i, 0)),
                  pl.BlockSpec((1, d, TF), lambda i, f: (j, 0, f)),
                  pl.BlockSpec((1, d, TF), lambda i, f: (j, 0, f)),
                  pl.BlockSpec((1, TF, d), lambda i, f: (j, f, 0)),
                  pl.BlockSpec((tm, d), lambda i, f: (i, 0), pipeline_mode=once),
                  _mod_spec(layer, k_gate, tm, n_ctx, lat_len)],
        out_specs=pl.BlockSpec((tm, d), lambda i, f: (i, 0), pipeline_mode=once),
        out_shape=jax.ShapeDtypeStruct((nt, d), F32),
        scratch_shapes=[pltpu.VMEM((tm, d), F32)],
        compiler_params=_params(("parallel", "arbitrary"), vmem),
        name="ffn_dense",
    )(h, w1, w3, w2, x, mods)


def _dispatch_kernel(dest_ref, hp_ref, init_ref, xs_ref, sem):
    del init_ref
    n = hp_ref.shape[0]

    def row_copy(r, dst_row):
        return pltpu.make_async_copy(hp_ref.at[r], xs_ref.at[dst_row], sem)

    def start(r, carry):
        row_copy(r, dest_ref[0, r]).start()
        row_copy(r, dest_ref[1, r]).start()
        return carry

    def wait(r, carry):
        row_copy(r, 0).wait()
        row_copy(r, 0).wait()
        return carry

    lax.fori_loop(0, n, start, 0, unroll=8)
    lax.fori_loop(0, n, wait, 0, unroll=8)


def _dispatch(hp, dest, n_rows):
    nt = hp.shape[0]
    tile = hp.shape[1:]
    vmem = 2 * _nbytes((MOVE_ROWS,) + tile, hp.dtype)
    return pl.pallas_call(
        _dispatch_kernel,
        grid=(nt // MOVE_ROWS,),
        in_specs=[pl.BlockSpec((2, MOVE_ROWS), lambda i: (0, i), memory_space=pltpu.SMEM),
                  pl.BlockSpec((MOVE_ROWS,) + tile, lambda i: (i, 0, 0)),
                  pl.BlockSpec(memory_space=pl.ANY)],
        out_specs=pl.BlockSpec(memory_space=pl.ANY),
        out_shape=jax.ShapeDtypeStruct((n_rows,) + tile, hp.dtype),
        scratch_shapes=[pltpu.SemaphoreType.DMA(())],
        input_output_aliases={2: 0},
        compiler_params=_params(("arbitrary",), vmem),
        name="moe_dispatch",
    )(dest, hp, jnp.zeros((n_rows,) + tile, hp.dtype))


def _moe_ffn_kernel(te_ref, nv_ref, hp_ref, w1_ref, w3_ref, w2_ref, o_ref, h_ref, acc_ref):
    i = pl.program_id(0)
    f = pl.program_id(1)
    last = pl.num_programs(1) - 1

    @pl.when(i < nv_ref[0])
    def _():
        @pl.when(f == 0)
        def _():
            half = h_ref.shape[1] // 2
            lo, hi = _unpack_bf16_pair(_tiles_to_blocks(hp_ref[...]))
            for q in range(ROW_TILES):
                h_ref[:, q * LANES:(q + 1) * LANES] = lo[q].astype(h_ref.dtype)
                h_ref[:, half + q * LANES:half + (q + 1) * LANES] = hi[q].astype(h_ref.dtype)
            acc_ref[...] = jnp.zeros_like(acc_ref)

        acc_ref[...] += _swiglu_partial(h_ref[...], w1_ref[0], w3_ref[0], w2_ref[0])

        @pl.when(f == last)
        def _():
            o_ref[...] = _rows_to_tiles(acc_ref[...])

    @pl.when(jnp.logical_and(i >= nv_ref[0], f == last))
    def _():
        o_ref[...] = jnp.zeros_like(o_ref)


def _moe_ffn(xs, w1, w3, w2, tile_expert, n_valid):
    r = xs.shape[0]
    d = D_MODEL
    ff = w1.shape[2]
    tm, tf = MOE_TM, MOE_TF
    vmem = (2 * (_nbytes((tm, d // 2), U32) + 3 * _nbytes((d, tf), MXU_DTYPE) + _nbytes((tm, d), F32))
            + _nbytes((tm, d), MXU_DTYPE) + _nbytes((tm, d), F32) + 3 * _nbytes((tm, tf), F32))

    def live(i, f, nv):
        return jnp.where(i < nv[0], f, 0)

    grid_spec = pltpu.PrefetchScalarGridSpec(
        num_scalar_prefetch=2,
        grid=(r // tm, ff // tf),
        in_specs=[pl.BlockSpec((tm, ROW_TILES, LANES), lambda i, f, te, nv: (i, 0, 0)),
                  pl.BlockSpec((1, d, tf), lambda i, f, te, nv: (te[i], 0, live(i, f, nv))),
                  pl.BlockSpec((1, d, tf), lambda i, f, te, nv: (te[i], 0, live(i, f, nv))),
                  pl.BlockSpec((1, tf, d), lambda i, f, te, nv: (te[i], live(i, f, nv), 0))],
        out_specs=pl.BlockSpec((tm, OUT_TILES, LANES), lambda i, f, te, nv: (i, 0, 0)),
        scratch_shapes=[pltpu.VMEM((tm, d), MXU_DTYPE), pltpu.VMEM((tm, d), F32)],
    )
    return pl.pallas_call(
        _moe_ffn_kernel,
        grid_spec=grid_spec,
        out_shape=jax.ShapeDtypeStruct((r, OUT_TILES, LANES), F32),
        compiler_params=_params(("arbitrary", "arbitrary"), vmem),
        name="moe_ffn",
    )(tile_expert, n_valid, xs, w1, w3, w2)


def _combine_kernel(dest_ref, ys_ref, p_ref, x_ref, g_ref, o_ref, buf_ref, sem):
    n = o_ref.shape[0]

    def row_copy(k, r, src_row):
        return pltpu.make_async_copy(ys_ref.at[src_row], buf_ref.at[k, r], sem)

    def start(r, carry):
        row_copy(0, r, dest_ref[0, r]).start()
        row_copy(1, r, dest_ref[1, r]).start()
        return carry

    def wait(r, carry):
        row_copy(0, r, 0).wait()
        row_copy(1, r, 0).wait()
        return carry

    lax.fori_loop(0, n, start, 0, unroll=8)
    lax.fori_loop(0, n, wait, 0, unroll=8)
    p1 = jnp.broadcast_to(p_ref[:, 0:1], (n, LANES))
    p2 = jnp.broadcast_to(p_ref[:, 1:2], (n, LANES))
    first, second = _tiles_to_blocks(buf_ref[0]), _tiles_to_blocks(buf_ref[1])
    for q in range(OUT_TILES):
        cols = slice(q * LANES, (q + 1) * LANES)
        o_ref[:, cols] = x_ref[:, cols] + g_ref[0, 0][:, cols] * (p1 * first[q] + p2 * second[q])


def _combine(ys, dest, probs, x, mods, layer, k_gate, n_ctx, lat_len):
    nt, d = x.shape
    tc = MOVE_ROWS
    vmem = 2 * (2 * _nbytes((tc, d), F32) + _nbytes((tc, LANES), F32)) + _nbytes((2, tc, d), F32)
    return pl.pallas_call(
        _combine_kernel,
        grid=(nt // tc,),
        in_specs=[pl.BlockSpec((2, tc), lambda i: (0, i), memory_space=pltpu.SMEM),
                  pl.BlockSpec(memory_space=pl.ANY),
                  pl.BlockSpec((tc, LANES), lambda i: (i, 0)),
                  pl.BlockSpec((tc, d), lambda i: (i, 0)),
                  _mod_spec(layer, k_gate, tc, n_ctx, lat_len)],
        out_specs=pl.BlockSpec((tc, d), lambda i: (i, 0)),
        out_shape=jax.ShapeDtypeStruct((nt, d), F32),
        scratch_shapes=[pltpu.VMEM((2, tc, OUT_TILES, LANES), F32), pltpu.SemaphoreType.DMA(())],
        compiler_params=_params(("arbitrary",), vmem),
        name="moe_combine",
    )(dest, ys, probs, x, mods)


def _route(idx, n_rows):
    nt = idx.shape[1]
    e_flat = idx.reshape(-1)
    onehot = (e_flat[:, None] == jnp.arange(N_EXPERTS)[None, :]).astype(jnp.int32)
    counts = jnp.sum(onehot, axis=0)
    rank = jnp.sum((jnp.cumsum(onehot, axis=0) - onehot) * onehot, axis=1)
    padded = ((counts + MOE_TM - 1) // MOE_TM) * MOE_TM
    ends = jnp.cumsum(padded)
    starts = ends - padded
    dest = jnp.sum(onehot * starts[None, :], axis=1) + rank
    tile_start = jnp.arange(n_rows // MOE_TM, dtype=jnp.int32) * MOE_TM
    tile_expert = jnp.minimum(jnp.sum((tile_start[:, None] >= ends[None, :]).astype(jnp.int32), axis=1),
                              N_EXPERTS - 1).astype(jnp.int32)
    n_valid = (ends[-1] // MOE_TM).astype(jnp.int32).reshape(1)
    return dest.reshape(2, nt).astype(jnp.int32), tile_expert, n_valid


def _moe(x, norm_g, mods, layer, w_r_t, b_r, w1, w3, w2, j, n_ctx, lat_len):
    nt = x.shape[0]
    hp, idx, probs = _norm_router(x, norm_g, mods, layer, 4, 3, w_r_t, b_r, j, n_ctx, lat_len)
    n_rows = 2 * nt + N_EXPERTS * MOE_TM
    dest, tile_expert, n_valid = _route(idx, n_rows)
    xs = _dispatch(hp, dest, n_rows)
    d = x.shape[1]
    ys = _moe_ffn(xs, w1.reshape(N_EXPERTS, d, -1), w3.reshape(N_EXPERTS, d, -1), w2.reshape(N_EXPERTS, -1, d),
                  tile_expert, n_valid)
    return _combine(ys, dest, probs, x, mods, layer, 5, n_ctx, lat_len)


def _mixer(x, mods, layer, wts, dims, moe_f32, moe_cast):
    n_ctx, ctx_seqs, ctx_len, lat_seqs, lat_len = dims
    nt = x.shape[0]
    h = _norm_mod(x, wts["norm1"], mods, layer, 1, 0, n_ctx, lat_len)
    z = _mm(h, wts["w_in_rest"], layer, F32, 3 * BRANCH_W, name="in_proj")
    if layer % 2 == 0:
        gates, moe_cast = _gate_proj_with_cast(h, wts["w_gate"], layer, moe_f32)
    else:
        gates = _mm(h, wts["w_gate"], layer, MXU_DTYPE, 2 * BRANCH_W, act="sigmoid", name="gate_proj")

    u_t = _mm_nt(wts["w_in_ssm_t"], h, layer).reshape(SSM_GROUPS, SSM_GROUP, nt // CHUNK, CHUNK)
    y_t, fin = _ssm(u_t, wts["ssm"], layer, ctx_seqs, ctx_len // CHUNK, lat_seqs, lat_len // CHUNK)
    br_ssm = _glu(y_t.reshape(BRANCH_W, nt), wts["w_glu_t"], layer)

    br_fft = jnp.concatenate([_fft(z, 0, ctx_seqs, ctx_len), _fft(z, n_ctx, lat_seqs, lat_len)], axis=0)
    br_sgu = _sgu(z, wts["sgu_norm"], wts["w_s"], wts["sgu_bias"], layer)
    br_conv = _conv(z, wts["conv_taps"], layer, n_ctx, ctx_len)

    merged = _merge((br_ssm, br_fft, br_sgu, br_conv), gates, wts["w_up"], layer)
    x = _mm_residual(merged, wts["w_o"], layer, x, mods, 2, n_ctx, lat_len)
    return x, fin, moe_cast


def kernel(x_prompt, x_sample, c, state_ssm_re, state_ssm_im, c_ctx, norm1_g, norm2_g, w_ada, b_ada, w_in, w_gate,
           ssm_lam_re, ssm_lam_im, ssm_b_re, ssm_b_im, ssm_c_re, ssm_c_im, ssm_log_dt, ssm_d, ssm_w_glu,
           sgu_norm_g, sgu_w_spatial, sgu_b_spatial, conv_w, w_up, w_o, ffn_w1, ffn_w3, ffn_w2,
           moe_w_router, moe_b_router, moe_w1, moe_w3, moe_w2, final_norm_g):
    ctx_seqs, ctx_len, d = x_prompt.shape
    lat_seqs, lat_len, _ = x_sample.shape
    depth = w_in.shape[0]
    n_ctx = ctx_seqs * ctx_len
    dims = (n_ctx, ctx_seqs, ctx_len, lat_seqs, lat_len)
    assert n_ctx % lat_len == 0 and lat_len % TM_MM == 0 and n_ctx % TM_MM == 0 and 1 + lat_seqs <= COND_ROWS
    assert ctx_len % CHUNK == 0 and ctx_len & (ctx_len - 1) == 0 and lat_len % GRID_W == 0 and TM % ctx_len == 0

    x = jnp.concatenate([x_prompt.reshape(n_ctx, d), x_sample.reshape(lat_seqs * lat_len, d)], axis=0)
    cond = jnp.concatenate([c_ctx[None, :], c, jnp.zeros((COND_ROWS - 1 - lat_seqs, d), F32)], axis=0)
    mods = _ada(cond, w_ada, b_ada).reshape(depth, COND_ROWS * N_MOD, 1, d)

    cast = lambda w: w.astype(MXU_DTYPE)
    b1, b2, b4 = BRANCH_W, 2 * BRANCH_W, 4 * BRANCH_W
    head_w = BRANCH_W // SGU_HEADS
    wts = {
        "norm1": norm1_g.reshape(depth, 1, d),
        "w_in_rest": cast(jnp.concatenate([w_in[:, :, b2:b4], w_in[:, :, b1:b2], w_in[:, :, b4:]], axis=2)),
        "w_in_ssm_t": cast(jnp.swapaxes(w_in[:, :, :b1], 1, 2)),
        "w_gate": cast(w_gate),
        "ssm": _ssm_tables(ssm_lam_re, ssm_lam_im, ssm_b_re, ssm_b_im, ssm_c_re, ssm_c_im, ssm_log_dt, ssm_d,
                           state_ssm_re, state_ssm_im),
        "w_glu_t": cast(jnp.swapaxes(ssm_w_glu, 1, 2)),
        "sgu_norm": sgu_norm_g.reshape(depth, 1, BRANCH_W),
        "w_s": cast(sgu_w_spatial),
        "sgu_bias": jnp.repeat(jnp.swapaxes(sgu_b_spatial, 1, 2), head_w, axis=2),
        "conv_taps": jnp.pad(jnp.swapaxes(conv_w, 1, 2), ((0, 0), (0, 5), (0, 0))),
        "w_up": cast(w_up), "w_o": cast(w_o),
    }
    norm2 = norm2_g.reshape(depth, 1, d)
    pad_ff = D_FF_PAD - D_FF
    ffn_w = (cast(jnp.pad(ffn_w1, ((0, 0), (0, 0), (0, pad_ff)))),
             cast(jnp.pad(ffn_w3, ((0, 0), (0, 0), (0, pad_ff)))),
             cast(jnp.pad(ffn_w2, ((0, 0), (0, pad_ff), (0, 0)))))
    moe_f32 = (moe_w1, moe_w3, moe_w2)
    w_r_t = jnp.swapaxes(moe_w_router, 1, 2)
    b_r = moe_b_router[:, :, None]

    new_re, new_im = [], []
    p = SSM_STATE
    for i in range(depth):
        x, fin, moe_cast = _mixer(x, mods, i, wts, dims, moe_f32, None if i % 2 == 0 else moe_cast)
        fin = fin.transpose(1, 0, 2)
        new_re.append(jnp.stack([fin[..., :p], fin[..., 2 * p:3 * p]], axis=1))
        new_im.append(jnp.stack([fin[..., p:2 * p], fin[..., 3 * p:]], axis=1))
        j = i // 2
        if i % 2 == 0:
            h2 = _norm_mod(x, norm2, mods, i, 4, 3, n_ctx, lat_len)
            x = _ffn(h2, *ffn_w, j, x, mods, i, 5, n_ctx, lat_len)
        else:
            x = _moe(x, norm2, mods, i, w_r_t, b_r, *moe_cast, j, n_ctx, lat_len)

    y_prompt = _final_norm(x, final_norm_g, 0, n_ctx).reshape(ctx_seqs, ctx_len, d)
    y_sample = _final_norm(x, final_norm_g, n_ctx, lat_seqs * lat_len).reshape(lat_seqs, lat_len, d)
    return y_prompt, y_sample, jnp.stack(new_re, axis=1), jnp.stack(new_im, axis=1)
```

```python
import functools
import math

import jax
import jax.numpy as jnp
from jax import lax
from jax.experimental import pallas as pl
from jax.experimental.pallas import tpu as pltpu

F32 = jnp.float32
BF16 = jnp.bfloat16
U32 = jnp.uint32
MXU_DTYPE = BF16

D_MODEL = 2048
BRANCH_W = 512
N_BRANCH = 4
SSM_GROUP = 16
SSM_GROUPS = 32
SSM_STATE = 64
CHUNK = 128
SSM_CENTER = CHUNK // 2
SGU_HEADS = 4
GRID_W = 64
D_FF = 5504
D_FF_PAD = 5632
N_EXPERTS = 8
N_MOD = 6
EPS = 1e-6
COND_ROWS = 16
LANES = 128
ROW_TILES = D_MODEL // 2 // LANES
OUT_TILES = D_MODEL // LANES

TM = 512
TM_MM = 1024
TF = 512
MOE_TM = 512
MOE_TF = 1024
MOVE_ROWS = 256
VMEM_CAP = 60 * 1024 * 1024


def _params(sem, vmem_bytes):
    limit = int(min(VMEM_CAP, max(vmem_bytes + (12 << 20), 32 << 20)))
    return pltpu.CompilerParams(dimension_semantics=sem, vmem_limit_bytes=limit)


def _nbytes(shape, dtype):
    return math.prod(shape) * jnp.dtype(dtype).itemsize


def _seq_row(i, tm, n_ctx, lat_len):
    start = i * tm
    return jnp.where(start < n_ctx, 0, 1 + (start - n_ctx) // lat_len)


def _mod_spec(layer, k, tm, n_ctx, lat_len, grid_rank=1):
    def index(i, *_):
        return (layer, _seq_row(i, tm, n_ctx, lat_len) * N_MOD + k, 0, 0)
    return pl.BlockSpec((1, 1, 1, D_MODEL), index)


def _swap_halves(x):
    return pltpu.roll(x, SSM_STATE, axis=1)


def _ada_kernel(c_ref, w_ref, b_ref, o_ref):
    c = c_ref[...]
    s = (c * jax.nn.sigmoid(c)).astype(MXU_DTYPE)
    o_ref[0] = jnp.dot(s, w_ref[0].astype(MXU_DTYPE), preferred_element_type=F32) + b_ref[0]


def _ada(cond, w_ada, b_ada):
    depth, d, n = w_ada.shape
    tn = 1024
    vmem = 2 * (_nbytes((COND_ROWS, d), F32) + _nbytes((d, tn), F32) + _nbytes((COND_ROWS, tn), F32))
    return pl.pallas_call(
        _ada_kernel,
        grid=(depth, n // tn),
        in_specs=[pl.BlockSpec((COND_ROWS, d), lambda l, j: (0, 0)),
                  pl.BlockSpec((1, d, tn), lambda l, j: (l, 0, j)),
                  pl.BlockSpec((1, 1, tn), lambda l, j: (l, 0, j))],
        out_specs=pl.BlockSpec((1, COND_ROWS, tn), lambda l, j: (l, 0, j)),
        out_shape=jax.ShapeDtypeStruct((depth, COND_ROWS, n), F32),
        compiler_params=_params(("parallel", "parallel"), vmem),
        name="ada",
    )(cond, w_ada, b_ada.reshape(depth, 1, n))


def _rmsnorm(x, g):
    return x * lax.rsqrt(jnp.mean(x * x, axis=-1, keepdims=True) + EPS) * g


def _norm_mod_kernel(x_ref, g_ref, sc_ref, sh_ref, o_ref):
    y = _rmsnorm(x_ref[...], g_ref[0]) * (1.0 + sc_ref[0, 0]) + sh_ref[0, 0]
    o_ref[...] = y.astype(o_ref.dtype)


def _norm_kernel(x_ref, g_ref, o_ref):
    o_ref[...] = _rmsnorm(x_ref[...], g_ref[...]).astype(o_ref.dtype)


def _pack_bf16_pair(lo, hi):
    lo_b = pltpu.bitcast(lo.astype(BF16).astype(F32), U32)
    hi_b = pltpu.bitcast(hi.astype(BF16).astype(F32), U32)
    return (lo_b >> 16) | (hi_b & jnp.uint32(0xFFFF0000))


def _unpack_bf16_pair(p):
    lo = pltpu.bitcast(p << 16, F32)
    hi = pltpu.bitcast(p & jnp.uint32(0xFFFF0000), F32)
    return lo, hi


def _rows_to_tiles(x):
    blocks = jnp.stack([x[:, q * LANES:(q + 1) * LANES] for q in range(x.shape[1] // LANES)], axis=0)
    return pltpu.einshape("qrl->rql", blocks)


def _tiles_to_blocks(t):
    return pltpu.einshape("rql->qrl", t)


def _norm_router_kernel(x_ref, g_ref, sc_ref, sh_ref, wr_ref, br_ref, hp_ref, idx_ref, prob_ref):
    h = _rmsnorm(x_ref[...], g_ref[0]) * (1.0 + sc_ref[0, 0]) + sh_ref[0, 0]
    half = h.shape[1] // 2
    hp_ref[...] = _rows_to_tiles(_pack_bf16_pair(h[:, :half], h[:, half:]))
    logits = lax.dot_general(wr_ref[0], h, (((1,), (1,)), ((), ())),
                             precision=lax.Precision.HIGHEST,
                             preferred_element_type=F32) + br_ref[0]
    e_iota = lax.broadcasted_iota(jnp.int32, logits.shape, 0).astype(F32)
    none = float(N_EXPERTS)
    m1 = jnp.max(logits, axis=0, keepdims=True)
    i1 = jnp.min(jnp.where(logits == m1, e_iota, none), axis=0, keepdims=True)
    rest = jnp.where(e_iota == i1, -jnp.inf, logits)
    m2 = jnp.max(rest, axis=0, keepdims=True)
    i2 = jnp.min(jnp.where(rest == m2, e_iota, none), axis=0, keepdims=True)
    r = jnp.exp(m2 - m1)
    p1 = 1.0 / (1.0 + r)
    idx_ref[...] = jnp.concatenate([i1, i2], axis=0).astype(jnp.int32)
    rows = jnp.concatenate([p1, r * p1, jnp.zeros((LANES - 2, p1.shape[1]), F32)], axis=0)
    prob_ref[...] = rows.T


def _norm_mod(x, g, mods, layer, k_scale, k_shift, n_ctx, lat_len):
    nt, d = x.shape
    vmem = 2 * (_nbytes((TM, d), F32) + _nbytes((TM, d), MXU_DTYPE))
    return pl.pallas_call(
        _norm_mod_kernel,
        grid=(nt // TM,),
        in_specs=[pl.BlockSpec((TM, d), lambda i: (i, 0)),
                  pl.BlockSpec((1, 1, d), lambda i: (layer, 0, 0)),
                  _mod_spec(layer, k_scale, TM, n_ctx, lat_len),
                  _mod_spec(layer, k_shift, TM, n_ctx, lat_len)],
        out_specs=pl.BlockSpec((TM, d), lambda i: (i, 0)),
        out_shape=jax.ShapeDtypeStruct((nt, d), MXU_DTYPE),
        compiler_params=_params(("parallel",), vmem),
        name="norm_mod",
    )(x, g, mods, mods)


def _final_norm(x, g, row0, rows):
    d = x.shape[1]
    tile0 = row0 // TM
    vmem = 4 * _nbytes((TM, d), F32)
    return pl.pallas_call(
        _norm_kernel,
        grid=(rows // TM,),
        in_specs=[pl.BlockSpec((TM, d), lambda i: (tile0 + i, 0)),
                  pl.BlockSpec((1, d), lambda i: (0, 0))],
        out_specs=pl.BlockSpec((TM, d), lambda i: (i, 0)),
        out_shape=jax.ShapeDtypeStruct((rows, d), F32),
        compiler_params=_params(("parallel",), vmem),
        name="final_norm",
    )(x, g.reshape(1, d))


def _norm_router(x, g, mods, layer, k_scale, k_shift, w_r_t, b_r, j, n_ctx, lat_len):
    nt, d = x.shape
    vmem = 2 * (_nbytes((TM, d), F32) + _nbytes((TM, d // 2), U32) + _nbytes((TM, LANES), F32))
    return pl.pallas_call(
        _norm_router_kernel,
        grid=(nt // TM,),
        in_specs=[pl.BlockSpec((TM, d), lambda i: (i, 0)),
                  pl.BlockSpec((1, 1, d), lambda i: (layer, 0, 0)),
                  _mod_spec(layer, k_scale, TM, n_ctx, lat_len),
                  _mod_spec(layer, k_shift, TM, n_ctx, lat_len),
                  pl.BlockSpec((1, N_EXPERTS, d), lambda i: (j, 0, 0)),
                  pl.BlockSpec((1, N_EXPERTS, 1), lambda i: (j, 0, 0))],
        out_specs=[pl.BlockSpec((TM, ROW_TILES, LANES), lambda i: (i, 0, 0)),
                   pl.BlockSpec((2, TM), lambda i: (0, i)),
                   pl.BlockSpec((TM, LANES), lambda i: (i, 0))],
        out_shape=[jax.ShapeDtypeStruct((nt, ROW_TILES, LANES), U32),
                   jax.ShapeDtypeStruct((2, nt), jnp.int32),
                   jax.ShapeDtypeStruct((nt, LANES), F32)],
        compiler_params=_params(("parallel",), vmem),
        name="norm_router",
    )(x, g, mods, mods, w_r_t, b_r)


def _mm_kernel(x_ref, w_ref, o_ref, *, act):
    y = jnp.dot(x_ref[...], w_ref[0], preferred_element_type=F32)
    if act == "sigmoid":
        y = jax.nn.sigmoid(y)
    o_ref[...] = y.astype(o_ref.dtype)


def _mm(x, w, layer, out_dtype, tn, act=None, name="mm"):
    m, k = x.shape
    n = w.shape[2]
    vmem = 2 * (_nbytes((TM_MM, k), x.dtype) + _nbytes((k, tn), w.dtype) + _nbytes((TM_MM, tn), out_dtype))
    return pl.pallas_call(
        functools.partial(_mm_kernel, act=act),
        grid=(m // TM_MM, n // tn),
        in_specs=[pl.BlockSpec((TM_MM, k), lambda i, j: (i, 0)),
                  pl.BlockSpec((1, k, tn), lambda i, j: (layer, 0, j))],
        out_specs=pl.BlockSpec((TM_MM, tn), lambda i, j: (i, j)),
        out_shape=jax.ShapeDtypeStruct((m, n), out_dtype),
        compiler_params=_params(("parallel", "parallel"), vmem),
        name=name,
    )(x, w)


def _gate_kernel(x_ref, w_ref, s_ref, o_ref, c_ref):
    o_ref[...] = jax.nn.sigmoid(jnp.dot(x_ref[...], w_ref[0], preferred_element_type=F32)).astype(o_ref.dtype)
    c_ref[...] = s_ref[0].astype(c_ref.dtype)


def _gate_proj_with_cast(h, w_gate, layer, w_f32):
    m, k = h.shape
    n = w_gate.shape[2]
    tn = 2 * BRANCH_W
    n_i, n_j = m // TM_MM, n // tn
    steps = n_i * n_j
    jl = layer // 2
    flat = w_f32.reshape(w_f32.shape[0], -1, w_f32.shape[-1])
    cols = flat.shape[2]
    n_slabs = 1 << (steps.bit_length() - 1)
    bf16_rows = 16
    assert flat.shape[1] % n_slabs == 0 and (flat.shape[1] // n_slabs) % bf16_rows == 0
    rows = flat.shape[1] // n_slabs

    def slab(i, j):
        return jnp.minimum(i * n_j + j, n_slabs - 1)

    vmem = 2 * (_nbytes((TM_MM, k), h.dtype) + _nbytes((k, tn), w_gate.dtype) + _nbytes((TM_MM, tn), MXU_DTYPE)
                + _nbytes((rows, cols), F32) + _nbytes((rows, cols), MXU_DTYPE)) + _nbytes((TM_MM, tn), F32)
    return pl.pallas_call(
        _gate_kernel,
        grid=(n_i, n_j),
        in_specs=[pl.BlockSpec((TM_MM, k), lambda i, j: (i, 0)),
                  pl.BlockSpec((1, k, tn), lambda i, j: (layer, 0, j)),
                  pl.BlockSpec((1, rows, cols), lambda i, j: (jl, slab(i, j), 0))],
        out_specs=[pl.BlockSpec((TM_MM, tn), lambda i, j: (i, j)),
                   pl.BlockSpec((rows, cols), lambda i, j: (slab(i, j), 0))],
        out_shape=[jax.ShapeDtypeStruct((m, n), MXU_DTYPE), jax.ShapeDtypeStruct(flat.shape[1:], MXU_DTYPE)],
        compiler_params=_params(("arbitrary", "arbitrary"), vmem),
        name="gate_proj_cast",
    )(h, w_gate, flat)


def _mm_nt_kernel(w_ref, x_ref, o_ref):
    o_ref[...] = lax.dot_general(w_ref[0], x_ref[...], (((1,), (1,)), ((), ())),
                                 preferred_element_type=F32)


def _mm_nt(w_t, x, layer):
    m, k = x.shape
    n = w_t.shape[1]
    vmem = 2 * (_nbytes((TM_MM, k), x.dtype) + _nbytes((n, k), w_t.dtype) + _nbytes((n, TM_MM), F32))
    return pl.pallas_call(
        _mm_nt_kernel,
        grid=(m // TM_MM,),
        in_specs=[pl.BlockSpec((1, n, k), lambda i: (layer, 0, 0)),
                  pl.BlockSpec((TM_MM, k), lambda i: (i, 0))],
        out_specs=pl.BlockSpec((n, TM_MM), lambda i: (0, i)),
        out_shape=jax.ShapeDtypeStruct((n, m), F32),
        compiler_params=_params(("parallel",), vmem),
        name="ssm_proj",
    )(w_t, x)


def _mm_res_kernel(a_ref, w_ref, x_ref, g_ref, o_ref):
    y = jnp.dot(a_ref[...], w_ref[0], preferred_element_type=F32)
    o_ref[...] = x_ref[...] + g_ref[0, 0] * y


def _mm_residual(a, w, layer, x, mods, k_gate, n_ctx, lat_len):
    m, k = a.shape
    n = w.shape[2]
    vmem = 2 * (_nbytes((TM, k), a.dtype) + _nbytes((k, n), w.dtype) + 2 * _nbytes((TM, n), F32))
    return pl.pallas_call(
        _mm_res_kernel,
        grid=(m // TM,),
        in_specs=[pl.BlockSpec((TM, k), lambda i: (i, 0)),
                  pl.BlockSpec((1, k, n), lambda i: (layer, 0, 0)),
                  pl.BlockSpec((TM, n), lambda i: (i, 0)),
                  _mod_spec(layer, k_gate, TM, n_ctx, lat_len)],
        out_specs=pl.BlockSpec((TM, n), lambda i: (i, 0)),
        out_shape=jax.ShapeDtypeStruct((m, n), F32),
        compiler_params=_params(("parallel",), vmem),
        name="out_proj",
    )(a, w, x, mods)


PW_LF, PW_RF, PW_LB, PW_RB, PW_SF, PW_SB, PW_OF, PW_OB = range(8)
V_LF, V_LB, V_RF, V_RB, V_SF, V_SB, V_OF, V_OB = (2 * k for k in range(8))


def _ssm_kernel(u_ref, pw_ref, vec_ref, lam_ref, h0_ref, d_ref, y_ref, fin_ref,
                lf_ref, lb_ref, rf_ref, rb_ref, wst_ref, wout_ref, toep_ref, sf_ref, sb_ref, hf_ref, hb_ref,
                *, ctx_seqs, ctx_chunks, lat_seqs, lat_chunks):
    kdim = SSM_GROUP * CHUNK
    half = 2 * SSM_STATE

    def fill(dst_ref, lanes, table, vset):
        tab = pw_ref[0, 0, table]
        tab_sw = _swap_halves(tab)
        for j in range(SSM_GROUP):
            v1 = vec_ref[0, 0, vset, j:j + 1, :]
            v2 = vec_ref[0, 0, vset + 1, j:j + 1, :]
            dst_ref[j * CHUNK:(j + 1) * CHUNK, lanes] = (tab * v1 + tab_sw * v2).astype(dst_ref.dtype)

    fwd, bwd = slice(0, half), slice(half, 2 * half)
    fill(lf_ref, fwd, PW_LF, V_LF)
    fill(lb_ref, fwd, PW_LB, V_LB)
    fill(rf_ref, fwd, PW_RF, V_RF)
    fill(rb_ref, fwd, PW_RB, V_RB)
    fill(wst_ref, fwd, PW_SF, V_SF)
    fill(wst_ref, bwd, PW_SB, V_SB)
    fill(wout_ref, fwd, PW_OF, V_OF)
    fill(wout_ref, bwd, PW_OB, V_OB)

    row_s = lax.broadcasted_iota(jnp.int32, (CHUNK, kdim), 0)
    col_t = lax.broadcasted_iota(jnp.int32, (CHUNK, kdim), 1) & (CHUNK - 1)
    causal = col_t >= row_s
    anti = col_t <= row_s
    nt_dims = (((1,), (1,)), ((), ()))

    def build(j, carry):
        rows = pl.ds(pl.multiple_of(j * CHUNK, CHUNK), CHUNK)
        a = lax.dot_general(lf_ref[rows, :], rf_ref[...], nt_dims, preferred_element_type=F32)
        b = lax.dot_general(lb_ref[rows, :], rb_ref[...], nt_dims, preferred_element_type=F32)
        toep_ref[rows, :] = (jnp.where(causal, a, 0.0) + jnp.where(anti, b, 0.0)).astype(toep_ref.dtype)
        return carry

    lax.fori_loop(0, SSM_GROUP, build, 0, unroll=4)

    u = jnp.concatenate([u_ref[0, j] for j in range(SSM_GROUP)], axis=-1)
    ub = u.astype(MXU_DTYPE)
    s_all = jnp.dot(ub, wst_ref[...], preferred_element_type=F32)
    sf_ref[...] = s_all[:, fwd]
    sb_ref[...] = s_all[:, bwd]

    def advance(h, s, lanes):
        return h * lam_ref[0, 0, 0:1, lanes] + _swap_halves(h) * lam_ref[0, 0, 1:2, lanes] + s

    def chain(base, seqs, chunks, h_fwd, h_bwd):
        rows = lambda c: pl.ds(base + c, seqs, stride=chunks) if chunks > 1 else pl.ds(base, seqs)
        for c in range(chunks):
            hf_ref[rows(c), :] = h_fwd
            h_fwd = advance(h_fwd, sf_ref[rows(c), :], fwd)
        for c in reversed(range(chunks)):
            hb_ref[rows(c), :] = h_bwd
            h_bwd = advance(h_bwd, sb_ref[rows(c), :], bwd)
        return h_fwd, h_bwd

    zeros = jnp.zeros((ctx_seqs, half), F32)
    end_fwd, end_bwd = chain(0, ctx_seqs, ctx_chunks, zeros, zeros)
    fin_ref[0, :, fwd] = end_fwd
    fin_ref[0, :, bwd] = end_bwd
    chain(ctx_seqs * ctx_chunks, lat_seqs, lat_chunks, h0_ref[0, 0, :, fwd], h0_ref[0, 0, :, bwd])

    hin = jnp.concatenate([hf_ref[...], hb_ref[...]], axis=-1).astype(MXU_DTYPE)
    nblk = 4
    for q in range(SSM_GROUP // nblk):
        cols = slice(q * nblk * CHUNK, (q + 1) * nblk * CHUNK)
        y = (jnp.dot(ub, toep_ref[:, cols], preferred_element_type=F32)
             + lax.dot_general(hin, wout_ref[cols, :], nt_dims, preferred_element_type=F32)
             + u[:, cols] * d_ref[0, 0][:, cols])
        for r in range(nblk):
            y_ref[0, q * nblk + r] = y[:, r * CHUNK:(r + 1) * CHUNK]


def _ssm(u_t, tabs, layer, ctx_seqs, ctx_chunks, lat_seqs, lat_chunks):
    g, hh, rows, _ = u_t.shape
    kdim = SSM_GROUP * CHUNK
    ns = 4 * SSM_STATE
    half = 2 * SSM_STATE
    n_pw, n_vec = tabs["pw"].shape[2], tabs["vec"].shape[2]
    vmem = (2 * (2 * _nbytes((hh, rows, CHUNK), F32) + _nbytes((n_pw, CHUNK, LANES), F32)
                 + _nbytes((n_vec, SSM_GROUP, LANES), F32))
            + 4 * _nbytes((kdim, half), MXU_DTYPE) + 2 * _nbytes((kdim, ns), MXU_DTYPE)
            + _nbytes((kdim, kdim), MXU_DTYPE) + 2 * _nbytes((rows, ns), F32)
            + 3 * _nbytes((rows, kdim), F32))
    per_lg = lambda *shape: pl.BlockSpec((1, 1) + shape, lambda i: (layer, i) + (0,) * len(shape))
    per_g = lambda *shape: pl.BlockSpec((1,) + shape, lambda i: (i,) + (0,) * len(shape))
    return pl.pallas_call(
        functools.partial(_ssm_kernel, ctx_seqs=ctx_seqs, ctx_chunks=ctx_chunks,
                          lat_seqs=lat_seqs, lat_chunks=lat_chunks),
        grid=(g,),
        in_specs=[per_g(hh, rows, CHUNK),
                  per_lg(n_pw, CHUNK, LANES), per_lg(n_vec, SSM_GROUP, LANES),
                  per_lg(2, ns), per_lg(lat_seqs, ns), per_lg(1, kdim)],
        out_specs=[per_g(hh, rows, CHUNK), per_g(ctx_seqs, ns)],
        out_shape=[jax.ShapeDtypeStruct(u_t.shape, F32),
                   jax.ShapeDtypeStruct((g, ctx_seqs, ns), F32)],
        scratch_shapes=[pltpu.VMEM((kdim, half), MXU_DTYPE), pltpu.VMEM((kdim, half), MXU_DTYPE),
                        pltpu.VMEM((kdim, half), MXU_DTYPE), pltpu.VMEM((kdim, half), MXU_DTYPE),
                        pltpu.VMEM((kdim, ns), MXU_DTYPE), pltpu.VMEM((kdim, ns), MXU_DTYPE),
                        pltpu.VMEM((kdim, kdim), MXU_DTYPE),
                        pltpu.VMEM((rows, half), F32), pltpu.VMEM((rows, half), F32),
                        pltpu.VMEM((rows, half), F32), pltpu.VMEM((rows, half), F32)],
        compiler_params=_params(("parallel",), vmem),
        name="s5_chunked",
    )(u_t, tabs["pw"], tabs["vec"], tabs["lam"], tabs["h0"], tabs["d"])


def _cmul(ar, ai, br, bi):
    return ar * br - ai * bi, ar * bi + ai * br


def _ssm_tables(lam_re, lam_im, b_re, b_im, c_re, c_im, log_dt, d_skip, state_re, state_im):
    dt = jnp.exp(log_dt)[..., None]
    ea, eb = lam_re * dt, lam_im * dt
    mag = jnp.exp(ea)
    lbar_re, lbar_im = mag * jnp.cos(eb), mag * jnp.sin(eb)
    den = lam_re * lam_re + lam_im * lam_im
    nr, ni = lbar_re - 1.0, lbar_im
    coef_re = (nr * lam_re + ni * lam_im) / den
    coef_im = (ni * lam_re - nr * lam_im) / den
    bb_re, bb_im = _cmul(coef_re[..., None], coef_im[..., None], b_re, b_im)
    bb_re, bb_im = jnp.swapaxes(bb_re, 3, 4), jnp.swapaxes(bb_im, 3, 4)

    def power(dr, n):
        n = n.astype(F32)[None, None, :, None]
        m = jnp.exp(ea[:, dr, :, None, :] * n)
        ang = eb[:, dr, :, None, :] * n
        return jnp.concatenate([m * jnp.cos(ang), m * jnp.sin(ang)], axis=-1)

    pos = jnp.arange(CHUNK)
    ctr = SSM_CENTER
    pw = jnp.stack([power(0, ctr - pos), power(0, pos - ctr), power(1, pos - ctr), power(1, ctr - pos),
                    power(0, CHUNK - 1 - pos), power(1, pos), power(0, pos + 1), power(1, CHUNK - pos)],
                   axis=2)

    cat = lambda *xs: jnp.concatenate(xs, axis=-1)
    conj_rows = lambda re, im: [cat(re, -re), cat(-im, -im)]
    plain_rows = lambda re, im: [cat(re, re), cat(-im, im)]
    vec = jnp.stack(conj_rows(bb_re[:, 0], bb_im[:, 0]) + conj_rows(bb_re[:, 1], bb_im[:, 1])
                    + plain_rows(c_re[:, 0], c_im[:, 0]) + plain_rows(c_re[:, 1], c_im[:, 1])
                    + plain_rows(bb_re[:, 0], bb_im[:, 0]) + plain_rows(bb_re[:, 1], bb_im[:, 1])
                    + conj_rows(c_re[:, 0], c_im[:, 0]) + conj_rows(c_re[:, 1], c_im[:, 1]),
                   axis=2)

    pt = power(0, jnp.array([CHUNK])), power(1, jnp.array([CHUNK]))
    p = SSM_STATE
    lam = jnp.concatenate(
        [cat(pt[0][..., :p], pt[0][..., :p], pt[1][..., :p], pt[1][..., :p]),
         cat(-pt[0][..., p:], pt[0][..., p:], -pt[1][..., p:], pt[1][..., p:])], axis=2)
    d_rows = jnp.repeat(d_skip.reshape(-1, SSM_GROUPS, 1, SSM_GROUP), CHUNK, axis=-1)
    h0 = cat(state_re[:, :, 0], state_im[:, :, 0], state_re[:, :, 1], state_im[:, :, 1]).transpose(1, 2, 0, 3)
    return {"pw": pw, "vec": vec, "lam": lam, "d": d_rows, "h0": h0}


def _glu_kernel(y_ref, w_ref, o_ref):
    z = jax.nn.gelu(y_ref[...])
    gate = jax.nn.sigmoid(jnp.dot(w_ref[0], z.astype(MXU_DTYPE), preferred_element_type=F32))
    o_ref[...] = (z * gate).T.astype(o_ref.dtype)


def _glu(y_t, w_t, layer):
    w_, nt = y_t.shape
    vmem = 2 * (_nbytes((w_, TM), F32) + _nbytes((w_, w_), MXU_DTYPE) + _nbytes((TM, w_), MXU_DTYPE))
    return pl.pallas_call(
        _glu_kernel,
        grid=(nt // TM,),
        in_specs=[pl.BlockSpec((w_, TM), lambda i: (0, i)),
                  pl.BlockSpec((1, w_, w_), lambda i: (layer, 0, 0))],
        out_specs=pl.BlockSpec((TM, w_), lambda i: (i, 0)),
        out_shape=jax.ShapeDtypeStruct((nt, w_), MXU_DTYPE),
        compiler_params=_params(("parallel",), vmem),
        name="s5_glu",
    )(y_t, w_t)


Z_SGU_BLOCK, Z_FFT_BLOCK, Z_CONV_BLOCK = 0, 2, 3


def _fft_kernel(x_ref, cs_ref, dft_ref, o_ref, pq_ref):
    seq_len = x_ref.shape[0]

    @pl.when(pl.program_id(1) == 0)
    def _():
        pq = jnp.dot(x_ref[...].astype(MXU_DTYPE), cs_ref[...], preferred_element_type=F32)
        pq_ref[:seq_len, :] = pq[:, :BRANCH_W].astype(pq_ref.dtype)
        pq_ref[seq_len:, :] = pq[:, BRANCH_W:].astype(pq_ref.dtype)

    o_ref[...] = jnp.dot(dft_ref[...], pq_ref[...], preferred_element_type=F32).astype(o_ref.dtype)


def _dft_tables(seq_len):
    def cs(n):
        k = jnp.arange(n, dtype=jnp.int32)
        ang = ((k[:, None] * k[None, :]) % n).astype(F32) * (2.0 * math.pi / n)
        return jnp.cos(ang) / math.sqrt(n), jnp.sin(ang) / math.sqrt(n)

    cl, sl = cs(seq_len)
    cc, sc = cs(CHUNK)
    eye = jnp.eye(BRANCH_W // CHUNK, dtype=F32)
    chan = jnp.concatenate([jnp.kron(eye, cc), jnp.kron(eye, sc)], axis=1)
    return jnp.concatenate([cl, -sl], axis=1).astype(MXU_DTYPE), chan.astype(MXU_DTYPE)


def _fft(z, row0, seqs, seq_len):
    dft, chan = _dft_tables(seq_len)
    tm = min(seq_len, TM)
    vmem = (2 * (_nbytes((seq_len, BRANCH_W), F32) + _nbytes(chan.shape, MXU_DTYPE)
                 + _nbytes((tm, 2 * seq_len), MXU_DTYPE) + _nbytes((tm, BRANCH_W), MXU_DTYPE))
            + _nbytes((2 * seq_len, BRANCH_W), MXU_DTYPE) + _nbytes((seq_len, 2 * BRANCH_W), F32))
    seq0 = row0 // seq_len
    return pl.pallas_call(
        _fft_kernel,
        grid=(seqs, seq_len // tm),
        in_specs=[pl.BlockSpec((seq_len, BRANCH_W), lambda b, j: (seq0 + b, Z_FFT_BLOCK)),
                  pl.BlockSpec(chan.shape, lambda b, j: (0, 0)),
                  pl.BlockSpec((tm, 2 * seq_len), lambda b, j: (j, 0))],
        out_specs=pl.BlockSpec((tm, BRANCH_W), lambda b, j: (b * (seq_len // tm) + j, 0)),
        out_shape=jax.ShapeDtypeStruct((seqs * seq_len, BRANCH_W), MXU_DTYPE),
        scratch_shapes=[pltpu.VMEM((2 * seq_len, BRANCH_W), MXU_DTYPE)],
        compiler_params=_params(("parallel", "arbitrary"), vmem),
        name="fnet_dft",
    )(z, chan, dft)


def _sgu_kernel(z_ref, g_ref, ws_ref, bs_ref, o_ref):
    uv = jax.nn.gelu(z_ref[...])
    u = uv[:, :BRANCH_W]
    v = _rmsnorm(uv[:, BRANCH_W:], g_ref[0]).astype(MXU_DTYPE)
    head_w = BRANCH_W // SGU_HEADS
    for c in range(z_ref.shape[0] // CHUNK):
        rows = slice(c * CHUNK, (c + 1) * CHUNK)
        s = jnp.concatenate(
            [jnp.dot(ws_ref[0, h], v[rows, h * head_w:(h + 1) * head_w], preferred_element_type=F32)
             for h in range(SGU_HEADS)], axis=-1)
        o_ref[rows, :] = (u[rows, :] * (s + bs_ref[0])).astype(o_ref.dtype)


def _sgu(z, g, w_s, bias, layer):
    nt = z.shape[0]
    vmem = 2 * (_nbytes((TM, 2 * BRANCH_W), F32) + _nbytes((TM, BRANCH_W), MXU_DTYPE)) + 3 * _nbytes((TM, 2 * BRANCH_W), F32)
    return pl.pallas_call(
        _sgu_kernel,
        grid=(nt // TM,),
        in_specs=[pl.BlockSpec((TM, 2 * BRANCH_W), lambda i: (i, Z_SGU_BLOCK)),
                  pl.BlockSpec((1, 1, BRANCH_W), lambda i: (layer, 0, 0)),
                  pl.BlockSpec((1, SGU_HEADS, CHUNK, CHUNK), lambda i: (layer, 0, 0, 0)),
                  pl.BlockSpec((1, CHUNK, BRANCH_W), lambda i: (layer, 0, 0))],
        out_specs=pl.BlockSpec((TM, BRANCH_W), lambda i: (i, 0)),
        out_shape=jax.ShapeDtypeStruct((nt, BRANCH_W), MXU_DTYPE),
        compiler_params=_params(("parallel",), vmem),
        name="sgu",
    )(z, g, w_s, bias)


def _conv_kernel(x_ref, b_ref, c_ref, w_ref, o_ref, *, n_ctx_tiles, ctx_period, lat_period):
    t = c_ref[...] * x_ref[...]
    tm = t.shape[0]
    period = jnp.where(pl.program_id(0) < n_ctx_tiles, ctx_period, lat_period)
    pos = lax.broadcasted_iota(jnp.int32, t.shape, 0) & (period - 1)
    prev = jnp.where(pos == 0, 0.0, pltpu.roll(t, 1, axis=0))
    nxt = jnp.where(pos == period - 1, 0.0, pltpu.roll(t, tm - 1, axis=0))
    w = w_ref[0]
    o_ref[...] = (b_ref[...] * (prev * w[0:1] + t * w[1:2] + nxt * w[2:3])).astype(o_ref.dtype)


def _conv(z, w_taps, layer, n_ctx, ctx_period):
    nt = z.shape[0]
    col = lambda k: pl.BlockSpec((TM, BRANCH_W), lambda i: (i, Z_CONV_BLOCK + k))
    vmem = 2 * (3 * _nbytes((TM, BRANCH_W), F32) + _nbytes((TM, BRANCH_W), MXU_DTYPE)) + 4 * _nbytes((TM, BRANCH_W), F32)
    return pl.pallas_call(
        functools.partial(_conv_kernel, n_ctx_tiles=n_ctx // TM, ctx_period=ctx_period, lat_period=GRID_W),
        grid=(nt // TM,),
        in_specs=[col(0), col(1), col(2),
                  pl.BlockSpec((1,) + w_taps.shape[1:], lambda i: (layer, 0, 0))],
        out_specs=pl.BlockSpec((TM, BRANCH_W), lambda i: (i, 0)),
        out_shape=jax.ShapeDtypeStruct((nt, BRANCH_W), MXU_DTYPE),
        compiler_params=_params(("parallel",), vmem),
        name="short_conv",
    )(z, z, z, w_taps)


def _merge_kernel(b0_ref, b1_ref, b2_ref, b3_ref, gate_ref, wup_ref, o_ref):
    acc = None
    for n, b_ref in enumerate((b0_ref, b1_ref, b2_ref, b3_ref)):
        up = jnp.dot(b_ref[...], wup_ref[0, n], preferred_element_type=F32)
        term = gate_ref[:, n * D_MODEL:(n + 1) * D_MODEL].astype(F32) * up
        acc = term if acc is None else acc + term
    o_ref[...] = acc.astype(o_ref.dtype)


def _merge(branches, gates, w_up, layer):
    nt = gates.shape[0]
    vmem = (2 * (4 * _nbytes((TM, BRANCH_W), MXU_DTYPE) + _nbytes((TM, N_BRANCH * D_MODEL), MXU_DTYPE)
                 + _nbytes(w_up.shape[1:], MXU_DTYPE) + _nbytes((TM, D_MODEL), MXU_DTYPE))
            + 2 * _nbytes((TM, D_MODEL), F32))
    br = pl.BlockSpec((TM, BRANCH_W), lambda i: (i, 0))
    return pl.pallas_call(
        _merge_kernel,
        grid=(nt // TM,),
        in_specs=[br, br, br, br,
                  pl.BlockSpec((TM, N_BRANCH * D_MODEL), lambda i: (i, 0)),
                  pl.BlockSpec((1,) + w_up.shape[1:], lambda i: (layer, 0, 0, 0))],
        out_specs=pl.BlockSpec((TM, D_MODEL), lambda i: (i, 0)),
        out_shape=jax.ShapeDtypeStruct((nt, D_MODEL), MXU_DTYPE),
        compiler_params=_params(("parallel",), vmem),
        name="branch_merge",
    )(*branches, gates, w_up)


def _swiglu_partial(h, w1, w3, w2):
    a = jnp.dot(h, w1, preferred_element_type=F32)
    b = jnp.dot(h, w3, preferred_element_type=F32)
    act = (a * jax.nn.sigmoid(a) * b).astype(MXU_DTYPE)
    return jnp.dot(act, w2, preferred_element_type=F32)


def _ffn_kernel(h_ref, w1_ref, w3_ref, w2_ref, x_ref, g_ref, o_ref, acc_ref):
    f = pl.program_id(1)

    @pl.when(f == 0)
    def _():
        acc_ref[...] = jnp.zeros_like(acc_ref)

    acc_ref[...] += _swiglu_partial(h_ref[...], w1_ref[0], w3_ref[0], w2_ref[0])

    @pl.when(f == pl.num_programs(1) - 1)
    def _():
        o_ref[...] = x_ref[...] + g_ref[0, 0] * acc_ref[...]


def _ffn(h, w1, w3, w2, j, x, mods, layer, k_gate, n_ctx, lat_len):
    nt, d = h.shape
    ff = w1.shape[2]
    tm = TM_MM
    once = pl.Buffered(1)
    vmem = (2 * (_nbytes((tm, d), MXU_DTYPE) + 3 * _nbytes((d, TF), MXU_DTYPE)) + 3 * _nbytes((tm, d), F32)
            + 3 * _nbytes((tm, TF), F32))
    return pl.pallas_call(
        _ffn_kernel,
        grid=(nt // tm, ff // TF),
        in_specs=[pl.BlockSpec((tm, d), lambda i, f: (i, 0)),
                  pl.BlockSpec((1, d, TF), lambda i, f: (j, 0, f)),
                  pl.BlockSpec((1, d, TF), lambda i, f: (j, 0, f)),
                  pl.BlockSpec((1, TF, d), lambda i, f: (j, f, 0)),
                  pl.BlockSpec((tm, d), lambda i, f: (i, 0), pipeline_mode=once),
                  _mod_spec(layer, k_gate, tm, n_ctx, lat_len)],
        out_specs=pl.BlockSpec((tm, d), lambda i, f: (i, 0), pipeline_mode=once),
        out_shape=jax.ShapeDtypeStruct((nt, d), F32),
        scratch_shapes=[pltpu.VMEM((tm, d), F32)],
        compiler_params=_params(("parallel", "arbitrary"), vmem),
        name="ffn_dense",
    )(h, w1, w3, w2, x, mods)


def _dispatch_kernel(dest_ref, hp_ref, init_ref, xs_ref, sem):
    del init_ref
    n = hp_ref.shape[0]

    def row_copy(r, dst_row):
        return pltpu.make_async_copy(hp_ref.at[r], xs_ref.at[dst_row], sem)

    def start(r, carry):
        row_copy(r, dest_ref[0, r]).start()
        row_copy(r, dest_ref[1, r]).start()
        return carry

    def wait(r, carry):
        row_copy(r, 0).wait()
        row_copy(r, 0).wait()
        return carry

    lax.fori_loop(0, n, start, 0, unroll=8)
    lax.fori_loop(0, n, wait, 0, unroll=8)


def _dispatch(hp, dest, n_rows):
    nt = hp.shape[0]
    tile = hp.shape[1:]
    vmem = 2 * _nbytes((MOVE_ROWS,) + tile, hp.dtype)
    return pl.pallas_call(
        _dispatch_kernel,
        grid=(nt // MOVE_ROWS,),
        in_specs=[pl.BlockSpec((2, MOVE_ROWS), lambda i: (0, i), memory_space=pltpu.SMEM),
                  pl.BlockSpec((MOVE_ROWS,) + tile, lambda i: (i, 0, 0)),
                  pl.BlockSpec(memory_space=pl.ANY)],
        out_specs=pl.BlockSpec(memory_space=pl.ANY),
        out_shape=jax.ShapeDtypeStruct((n_rows,) + tile, hp.dtype),
        scratch_shapes=[pltpu.SemaphoreType.DMA(())],
        input_output_aliases={2: 0},
        compiler_params=_params(("arbitrary",), vmem),
        name="moe_dispatch",
    )(dest, hp, jnp.zeros((n_rows,) + tile, hp.dtype))


def _moe_ffn_kernel(te_ref, nv_ref, hp_ref, w1_ref, w3_ref, w2_ref, o_ref, h_ref, acc_ref):
    i = pl.program_id(0)
    f = pl.program_id(1)
    last = pl.num_programs(1) - 1

    @pl.when(i < nv_ref[0])
    def _():
        @pl.when(f == 0)
        def _():
            half = h_ref.shape[1] // 2
            lo, hi = _unpack_bf16_pair(_tiles_to_blocks(hp_ref[...]))
            for q in range(ROW_TILES):
                h_ref[:, q * LANES:(q + 1) * LANES] = lo[q].astype(h_ref.dtype)
                h_ref[:, half + q * LANES:half + (q + 1) * LANES] = hi[q].astype(h_ref.dtype)
            acc_ref[...] = jnp.zeros_like(acc_ref)

        acc_ref[...] += _swiglu_partial(h_ref[...], w1_ref[0], w3_ref[0], w2_ref[0, 0])

        @pl.when(f == last)
        def _():
            o_ref[...] = _rows_to_tiles(acc_ref[...])

    @pl.when(jnp.logical_and(i >= nv_ref[0], f == last))
    def _():
        o_ref[...] = jnp.zeros_like(o_ref)


def _moe_ffn(xs, w1, w3, w2, j, tile_expert, n_valid):
    r = xs.shape[0]
    d = D_MODEL
    ff = w1.shape[2]
    tm, tf = MOE_TM, MOE_TF
    vmem = (2 * (_nbytes((tm, d // 2), U32) + 3 * _nbytes((d, tf), MXU_DTYPE) + _nbytes((tm, d), F32))
            + _nbytes((tm, d), MXU_DTYPE) + _nbytes((tm, d), F32) + 3 * _nbytes((tm, tf), F32))

    def live(i, f, nv):
        return jnp.where(i < nv[0], f, 0)

    grid_spec = pltpu.PrefetchScalarGridSpec(
        num_scalar_prefetch=2,
        grid=(r // tm, ff // tf),
        in_specs=[pl.BlockSpec((tm, ROW_TILES, LANES), lambda i, f, te, nv: (i, 0, 0)),
                  pl.BlockSpec((1, d, tf), lambda i, f, te, nv: (te[i], 0, live(i, f, nv))),
                  pl.BlockSpec((1, d, tf), lambda i, f, te, nv: (te[i], 0, live(i, f, nv))),
                  pl.BlockSpec((1, 1, tf, d), lambda i, f, te, nv: (j, te[i], live(i, f, nv), 0))],
        out_specs=pl.BlockSpec((tm, OUT_TILES, LANES), lambda i, f, te, nv: (i, 0, 0)),
        scratch_shapes=[pltpu.VMEM((tm, d), MXU_DTYPE), pltpu.VMEM((tm, d), F32)],
    )
    return pl.pallas_call(
        _moe_ffn_kernel,
        grid_spec=grid_spec,
        out_shape=jax.ShapeDtypeStruct((r, OUT_TILES, LANES), F32),
        compiler_params=_params(("arbitrary", "arbitrary"), vmem),
        name="moe_ffn",
    )(tile_expert, n_valid, xs, w1, w3, w2)


def _combine_kernel(dest_ref, next_ref, ys_ref, p_ref, x_ref, g_ref, o_ref, buf_ref, sems):
    step = pl.program_id(0)
    n = o_ref.shape[0]
    slot = lax.rem(step, 2)

    def row_copy(s, k, r, src_row):
        return pltpu.make_async_copy(ys_ref.at[src_row], buf_ref.at[s, k, r], sems.at[s])

    def request(idx_ref, s):
        def start(r, carry):
            row_copy(s, 0, r, idx_ref[0, r]).start()
            row_copy(s, 1, r, idx_ref[1, r]).start()
            return carry
        lax.fori_loop(0, n, start, 0, unroll=8)

    @pl.when(step == 0)
    def _():
        request(dest_ref, 0)

    @pl.when(step + 1 < pl.num_programs(0))
    def _():
        request(next_ref, 1 - slot)

    def wait(r, carry):
        row_copy(slot, 0, r, 0).wait()
        row_copy(slot, 1, r, 0).wait()
        return carry

    lax.fori_loop(0, n, wait, 0, unroll=8)
    p1 = jnp.broadcast_to(p_ref[:, 0:1], (n, LANES))
    p2 = jnp.broadcast_to(p_ref[:, 1:2], (n, LANES))
    first, second = _tiles_to_blocks(buf_ref[slot, 0]), _tiles_to_blocks(buf_ref[slot, 1])
    for q in range(OUT_TILES):
        cols = slice(q * LANES, (q + 1) * LANES)
        o_ref[:, cols] = x_ref[:, cols] + g_ref[0, 0][:, cols] * (p1 * first[q] + p2 * second[q])


def _combine(ys, dest, probs, x, mods, layer, k_gate, n_ctx, lat_len):
    nt, d = x.shape
    tc = MOVE_ROWS
    n_steps = nt // tc
    vmem = 2 * (2 * _nbytes((tc, d), F32) + _nbytes((tc, LANES), F32)) + 3 * _nbytes((2, tc, d), F32)
    return pl.pallas_call(
        _combine_kernel,
        grid=(n_steps,),
        in_specs=[pl.BlockSpec((2, tc), lambda i: (0, i), memory_space=pltpu.SMEM),
                  pl.BlockSpec((2, tc), lambda i: (0, jnp.minimum(i + 1, n_steps - 1)), memory_space=pltpu.SMEM),
                  pl.BlockSpec(memory_space=pl.ANY),
                  pl.BlockSpec((tc, LANES), lambda i: (i, 0)),
                  pl.BlockSpec((tc, d), lambda i: (i, 0)),
                  _mod_spec(layer, k_gate, tc, n_ctx, lat_len)],
        out_specs=pl.BlockSpec((tc, d), lambda i: (i, 0)),
        out_shape=jax.ShapeDtypeStruct((nt, d), F32),
        scratch_shapes=[pltpu.VMEM((2, 2, tc, OUT_TILES, LANES), F32), pltpu.SemaphoreType.DMA((2,))],
        compiler_params=_params(("arbitrary",), vmem),
        name="moe_combine",
    )(dest, dest, ys, probs, x, mods)


def _route(idx, n_rows):
    nt = idx.shape[1]
    e_flat = idx.reshape(-1)
    onehot = (e_flat[:, None] == jnp.arange(N_EXPERTS)[None, :]).astype(jnp.int32)
    counts = jnp.sum(onehot, axis=0)
    rank = jnp.sum((jnp.cumsum(onehot, axis=0) - onehot) * onehot, axis=1)
    padded = ((counts + MOE_TM - 1) // MOE_TM) * MOE_TM
    ends = jnp.cumsum(padded)
    starts = ends - padded
    dest = jnp.sum(onehot * starts[None, :], axis=1) + rank
    tile_start = jnp.arange(n_rows // MOE_TM, dtype=jnp.int32) * MOE_TM
    tile_expert = jnp.minimum(jnp.sum((tile_start[:, None] >= ends[None, :]).astype(jnp.int32), axis=1),
                              N_EXPERTS - 1).astype(jnp.int32)
    n_valid = (ends[-1] // MOE_TM).astype(jnp.int32).reshape(1)
    return dest.reshape(2, nt).astype(jnp.int32), tile_expert, n_valid


def _moe(x, norm_g, mods, layer, w_r_t, b_r, w1, w3, w2, j, n_ctx, lat_len):
    nt = x.shape[0]
    hp, idx, probs = _norm_router(x, norm_g, mods, layer, 4, 3, w_r_t, b_r, j, n_ctx, lat_len)
    n_rows = 2 * nt + N_EXPERTS * MOE_TM
    dest, tile_expert, n_valid = _route(idx, n_rows)
    xs = _dispatch(hp, dest, n_rows)
    d = x.shape[1]
    ys = _moe_ffn(xs, w1.reshape(N_EXPERTS, d, -1), w3.reshape(N_EXPERTS, d, -1), w2, j, tile_expert, n_valid)
    return _combine(ys, dest, probs, x, mods, layer, 5, n_ctx, lat_len)


def _mixer(x, mods, layer, wts, dims, moe_w_f32):
    n_ctx, ctx_seqs, ctx_len, lat_seqs, lat_len = dims
    nt = x.shape[0]
    h = _norm_mod(x, wts["norm1"], mods, layer, 1, 0, n_ctx, lat_len)
    z = _mm(h, wts["w_in_rest"], layer, F32, 3 * BRANCH_W, name="in_proj")
    gates, moe_w_cast = _gate_proj_with_cast(h, wts["w_gate"], layer, moe_w_f32)

    u_t = _mm_nt(wts["w_in_ssm_t"], h, layer).reshape(SSM_GROUPS, SSM_GROUP, nt // CHUNK, CHUNK)
    y_t, fin = _ssm(u_t, wts["ssm"], layer, ctx_seqs, ctx_len // CHUNK, lat_seqs, lat_len // CHUNK)
    br_ssm = _glu(y_t.reshape(BRANCH_W, nt), wts["w_glu_t"], layer)

    br_fft = jnp.concatenate([_fft(z, 0, ctx_seqs, ctx_len), _fft(z, n_ctx, lat_seqs, lat_len)], axis=0)
    br_sgu = _sgu(z, wts["sgu_norm"], wts["w_s"], wts["sgu_bias"], layer)
    br_conv = _conv(z, wts["conv_taps"], layer, n_ctx, ctx_len)

    merged = _merge((br_ssm, br_fft, br_sgu, br_conv), gates, wts["w_up"], layer)
    x = _mm_residual(merged, wts["w_o"], layer, x, mods, 2, n_ctx, lat_len)
    return x, fin, moe_w_cast


def kernel(x_prompt, x_sample, c, state_ssm_re, state_ssm_im, c_ctx, norm1_g, norm2_g, w_ada, b_ada, w_in, w_gate,
           ssm_lam_re, ssm_lam_im, ssm_b_re, ssm_b_im, ssm_c_re, ssm_c_im, ssm_log_dt, ssm_d, ssm_w_glu,
           sgu_norm_g, sgu_w_spatial, sgu_b_spatial, conv_w, w_up, w_o, ffn_w1, ffn_w3, ffn_w2,
           moe_w_router, moe_b_router, moe_w1, moe_w3, moe_w2, final_norm_g):
    ctx_seqs, ctx_len, d = x_prompt.shape
    lat_seqs, lat_len, _ = x_sample.shape
    depth = w_in.shape[0]
    n_ctx = ctx_seqs * ctx_len
    dims = (n_ctx, ctx_seqs, ctx_len, lat_seqs, lat_len)
    assert n_ctx % lat_len == 0 and lat_len % TM_MM == 0 and n_ctx % TM_MM == 0 and 1 + lat_seqs <= COND_ROWS
    assert ctx_len % CHUNK == 0 and ctx_len & (ctx_len - 1) == 0 and lat_len % GRID_W == 0 and TM % ctx_len == 0

    x = jnp.concatenate([x_prompt.reshape(n_ctx, d), x_sample.reshape(lat_seqs * lat_len, d)], axis=0)
    cond = jnp.concatenate([c_ctx[None, :], c, jnp.zeros((COND_ROWS - 1 - lat_seqs, d), F32)], axis=0)
    mods = _ada(cond, w_ada, b_ada).reshape(depth, COND_ROWS * N_MOD, 1, d)

    cast = lambda w: w.astype(MXU_DTYPE)
    b1, b2, b4 = BRANCH_W, 2 * BRANCH_W, 4 * BRANCH_W
    head_w = BRANCH_W // SGU_HEADS
    wts = {
        "norm1": norm1_g.reshape(depth, 1, d),
        "w_in_rest": cast(jnp.concatenate([w_in[:, :, b2:b4], w_in[:, :, b1:b2], w_in[:, :, b4:]], axis=2)),
        "w_in_ssm_t": cast(jnp.swapaxes(w_in[:, :, :b1], 1, 2)),
        "w_gate": cast(w_gate),
        "ssm": _ssm_tables(ssm_lam_re, ssm_lam_im, ssm_b_re, ssm_b_im, ssm_c_re, ssm_c_im, ssm_log_dt, ssm_d,
                           state_ssm_re, state_ssm_im),
        "w_glu_t": cast(jnp.swapaxes(ssm_w_glu, 1, 2)),
        "sgu_norm": sgu_norm_g.reshape(depth, 1, BRANCH_W),
        "w_s": cast(sgu_w_spatial),
        "sgu_bias": jnp.repeat(jnp.swapaxes(sgu_b_spatial, 1, 2), head_w, axis=2),
        "conv_taps": jnp.pad(jnp.swapaxes(conv_w, 1, 2), ((0, 0), (0, 5), (0, 0))),
        "w_up": cast(w_up), "w_o": cast(w_o),
    }
    norm2 = norm2_g.reshape(depth, 1, d)
    pad_ff = D_FF_PAD - D_FF
    ffn_w = (cast(jnp.pad(ffn_w1, ((0, 0), (0, 0), (0, pad_ff)))),
             cast(jnp.pad(ffn_w3, ((0, 0), (0, 0), (0, pad_ff)))),
             cast(jnp.pad(ffn_w2, ((0, 0), (0, pad_ff), (0, 0)))))
    moe_w2_cast = cast(moe_w2)
    w_r_t = jnp.swapaxes(moe_w_router, 1, 2)
    b_r = moe_b_router[:, :, None]

    new_re, new_im = [], []
    p = SSM_STATE
    for i in range(depth):
        x, fin, w_cast = _mixer(x, mods, i, wts, dims, moe_w1 if i % 2 == 0 else moe_w3)
        moe_w13 = (w_cast,) if i % 2 == 0 else moe_w13 + (w_cast,)
        fin = fin.transpose(1, 0, 2)
        new_re.append(jnp.stack([fin[..., :p], fin[..., 2 * p:3 * p]], axis=1))
        new_im.append(jnp.stack([fin[..., p:2 * p], fin[..., 3 * p:]], axis=1))
        j = i // 2
        if i % 2 == 0:
            h2 = _norm_mod(x, norm2, mods, i, 4, 3, n_ctx, lat_len)
            x = _ffn(h2, *ffn_w, j, x, mods, i, 5, n_ctx, lat_len)
        else:
            x = _moe(x, norm2, mods, i, w_r_t, b_r, *moe_w13, moe_w2_cast, j, n_ctx, lat_len)

    y_prompt = _final_norm(x, final_norm_g, 0, n_ctx).reshape(ctx_seqs, ctx_len, d)
    y_sample = _final_norm(x, final_norm_g, n_ctx, lat_seqs * lat_len).reshape(lat_seqs, lat_len, d)
    return y_prompt, y_sample, jnp.stack(new_re, axis=1), jnp.stack(new_im, axis=1)
```

```python
import functools
import math

import jax
import jax.numpy as jnp
from jax import lax
from jax.experimental import pallas as pl
from jax.experimental.pallas import tpu as pltpu

F32 = jnp.float32
BF16 = jnp.bfloat16
U32 = jnp.uint32
MXU_DTYPE = BF16

D_MODEL = 2048
BRANCH_W = 512
N_BRANCH = 4
SSM_GROUP = 16
SSM_GROUPS = 32
SSM_STATE = 64
CHUNK = 128
SSM_CENTER = CHUNK // 2
SGU_HEADS = 4
GRID_W = 64
D_FF = 5504
D_FF_PAD = 5632
N_EXPERTS = 8
N_MOD = 6
EPS = 1e-6
COND_ROWS = 16
LANES = 128
ROW_TILES = D_MODEL // 2 // LANES
OUT_TILES = D_MODEL // LANES

TM = 512
TM_MM = 1024
TF = 512
MOE_TM = 512
MOE_TF = 1024
MOVE_ROWS = 256
VMEM_CAP = 60 * 1024 * 1024


def _params(sem, vmem_bytes):
    limit = int(min(VMEM_CAP, max(vmem_bytes + (12 << 20), 32 << 20)))
    return pltpu.CompilerParams(dimension_semantics=sem, vmem_limit_bytes=limit)


def _nbytes(shape, dtype):
    return math.prod(shape) * jnp.dtype(dtype).itemsize


def _seq_row(i, tm, n_ctx, lat_len):
    start = i * tm
    return jnp.where(start < n_ctx, 0, 1 + (start - n_ctx) // lat_len)


def _mod_spec(layer, k, tm, n_ctx, lat_len, grid_rank=1):
    def index(i, *_):
        return (layer, _seq_row(i, tm, n_ctx, lat_len) * N_MOD + k, 0, 0)
    return pl.BlockSpec((1, 1, 1, D_MODEL), index)


def _swap_halves(x):
    return pltpu.roll(x, SSM_STATE, axis=1)


def _ada_kernel(c_ref, w_ref, b_ref, o_ref):
    c = c_ref[...]
    s = (c * jax.nn.sigmoid(c)).astype(MXU_DTYPE)
    o_ref[0] = jnp.dot(s, w_ref[0].astype(MXU_DTYPE), preferred_element_type=F32) + b_ref[0]


def _ada(cond, w_ada, b_ada):
    depth, d, n = w_ada.shape
    tn = 1024
    vmem = 2 * (_nbytes((COND_ROWS, d), F32) + _nbytes((d, tn), F32) + _nbytes((COND_ROWS, tn), F32))
    return pl.pallas_call(
        _ada_kernel,
        grid=(depth, n // tn),
        in_specs=[pl.BlockSpec((COND_ROWS, d), lambda l, j: (0, 0)),
                  pl.BlockSpec((1, d, tn), lambda l, j: (l, 0, j)),
                  pl.BlockSpec((1, 1, tn), lambda l, j: (l, 0, j))],
        out_specs=pl.BlockSpec((1, COND_ROWS, tn), lambda l, j: (l, 0, j)),
        out_shape=jax.ShapeDtypeStruct((depth, COND_ROWS, n), F32),
        compiler_params=_params(("parallel", "parallel"), vmem),
        name="ada",
    )(cond, w_ada, b_ada.reshape(depth, 1, n))


def _rmsnorm(x, g):
    return x * lax.rsqrt(jnp.mean(x * x, axis=-1, keepdims=True) + EPS) * g


def _norm_mod_kernel(x_ref, g_ref, sc_ref, sh_ref, o_ref):
    y = _rmsnorm(x_ref[...], g_ref[0]) * (1.0 + sc_ref[0, 0]) + sh_ref[0, 0]
    o_ref[...] = y.astype(o_ref.dtype)


def _norm_kernel(x_ref, g_ref, o_ref):
    o_ref[...] = _rmsnorm(x_ref[...], g_ref[...]).astype(o_ref.dtype)


def _pack_bf16_pair(lo, hi):
    lo_b = pltpu.bitcast(lo.astype(BF16).astype(F32), U32)
    hi_b = pltpu.bitcast(hi.astype(BF16).astype(F32), U32)
    return (lo_b >> 16) | (hi_b & jnp.uint32(0xFFFF0000))


def _unpack_bf16_pair(p):
    lo = pltpu.bitcast(p << 16, F32)
    hi = pltpu.bitcast(p & jnp.uint32(0xFFFF0000), F32)
    return lo, hi


def _rows_to_tiles(x):
    blocks = jnp.stack([x[:, q * LANES:(q + 1) * LANES] for q in range(x.shape[1] // LANES)], axis=0)
    return pltpu.einshape("qrl->rql", blocks)


def _tiles_to_blocks(t):
    return pltpu.einshape("rql->qrl", t)


def _norm_router_kernel(x_ref, g_ref, sc_ref, sh_ref, wr_ref, br_ref, hp_ref, idx_ref, prob_ref):
    h = _rmsnorm(x_ref[...], g_ref[0]) * (1.0 + sc_ref[0, 0]) + sh_ref[0, 0]
    half = h.shape[1] // 2
    hp_ref[...] = _rows_to_tiles(_pack_bf16_pair(h[:, :half], h[:, half:]))
    logits = lax.dot_general(wr_ref[0], h, (((1,), (1,)), ((), ())),
                             precision=lax.Precision.HIGHEST,
                             preferred_element_type=F32) + br_ref[0]
    e_iota = lax.broadcasted_iota(jnp.int32, logits.shape, 0).astype(F32)
    none = float(N_EXPERTS)
    m1 = jnp.max(logits, axis=0, keepdims=True)
    i1 = jnp.min(jnp.where(logits == m1, e_iota, none), axis=0, keepdims=True)
    rest = jnp.where(e_iota == i1, -jnp.inf, logits)
    m2 = jnp.max(rest, axis=0, keepdims=True)
    i2 = jnp.min(jnp.where(rest == m2, e_iota, none), axis=0, keepdims=True)
    r = jnp.exp(m2 - m1)
    p1 = 1.0 / (1.0 + r)
    idx_ref[...] = jnp.concatenate([i1, i2], axis=0).astype(jnp.int32)
    rows = jnp.concatenate([p1, r * p1, jnp.zeros((LANES - 2, p1.shape[1]), F32)], axis=0)
    prob_ref[...] = rows.T


def _norm_mod(x, g, mods, layer, k_scale, k_shift, n_ctx, lat_len):
    nt, d = x.shape
    vmem = 2 * (_nbytes((TM, d), F32) + _nbytes((TM, d), MXU_DTYPE))
    return pl.pallas_call(
        _norm_mod_kernel,
        grid=(nt // TM,),
        in_specs=[pl.BlockSpec((TM, d), lambda i: (i, 0)),
                  pl.BlockSpec((1, 1, d), lambda i: (layer, 0, 0)),
                  _mod_spec(layer, k_scale, TM, n_ctx, lat_len),
                  _mod_spec(layer, k_shift, TM, n_ctx, lat_len)],
        out_specs=pl.BlockSpec((TM, d), lambda i: (i, 0)),
        out_shape=jax.ShapeDtypeStruct((nt, d), MXU_DTYPE),
        compiler_params=_params(("parallel",), vmem),
        name="norm_mod",
    )(x, g, mods, mods)


def _final_norm(x, g, row0, rows):
    d = x.shape[1]
    tile0 = row0 // TM
    vmem = 4 * _nbytes((TM, d), F32)
    return pl.pallas_call(
        _norm_kernel,
        grid=(rows // TM,),
        in_specs=[pl.BlockSpec((TM, d), lambda i: (tile0 + i, 0)),
                  pl.BlockSpec((1, d), lambda i: (0, 0))],
        out_specs=pl.BlockSpec((TM, d), lambda i: (i, 0)),
        out_shape=jax.ShapeDtypeStruct((rows, d), F32),
        compiler_params=_params(("parallel",), vmem),
        name="final_norm",
    )(x, g.reshape(1, d))


def _norm_router(x, g, mods, layer, k_scale, k_shift, w_r_t, b_r, j, n_ctx, lat_len):
    nt, d = x.shape
    vmem = 2 * (_nbytes((TM, d), F32) + _nbytes((TM, d // 2), U32) + _nbytes((TM, LANES), F32))
    return pl.pallas_call(
        _norm_router_kernel,
        grid=(nt // TM,),
        in_specs=[pl.BlockSpec((TM, d), lambda i: (i, 0)),
                  pl.BlockSpec((1, 1, d), lambda i: (layer, 0, 0)),
                  _mod_spec(layer, k_scale, TM, n_ctx, lat_len),
                  _mod_spec(layer, k_shift, TM, n_ctx, lat_len),
                  pl.BlockSpec((1, N_EXPERTS, d), lambda i: (j, 0, 0)),
                  pl.BlockSpec((1, N_EXPERTS, 1), lambda i: (j, 0, 0))],
        out_specs=[pl.BlockSpec((TM, ROW_TILES, LANES), lambda i: (i, 0, 0)),
                   pl.BlockSpec((2, TM), lambda i: (0, i)),
                   pl.BlockSpec((TM, LANES), lambda i: (i, 0))],
        out_shape=[jax.ShapeDtypeStruct((nt, ROW_TILES, LANES), U32),
                   jax.ShapeDtypeStruct((2, nt), jnp.int32),
                   jax.ShapeDtypeStruct((nt, LANES), F32)],
        compiler_params=_params(("parallel",), vmem),
        name="norm_router",
    )(x, g, mods, mods, w_r_t, b_r)


def _mm_kernel(x_ref, w_ref, o_ref, *, act):
    y = jnp.dot(x_ref[...], w_ref[0], preferred_element_type=F32)
    if act == "sigmoid":
        y = jax.nn.sigmoid(y)
    o_ref[...] = y.astype(o_ref.dtype)


def _mm(x, w, layer, out_dtype, tn, act=None, name="mm"):
    m, k = x.shape
    n = w.shape[2]
    vmem = 2 * (_nbytes((TM_MM, k), x.dtype) + _nbytes((k, tn), w.dtype) + _nbytes((TM_MM, tn), out_dtype))
    return pl.pallas_call(
        functools.partial(_mm_kernel, act=act),
        grid=(m // TM_MM, n // tn),
        in_specs=[pl.BlockSpec((TM_MM, k), lambda i, j: (i, 0)),
                  pl.BlockSpec((1, k, tn), lambda i, j: (layer, 0, j))],
        out_specs=pl.BlockSpec((TM_MM, tn), lambda i, j: (i, j)),
        out_shape=jax.ShapeDtypeStruct((m, n), out_dtype),
        compiler_params=_params(("parallel", "parallel"), vmem),
        name=name,
    )(x, w)


def _gate_kernel(x_ref, w_ref, s_ref, o_ref, c_ref):
    o_ref[...] = jax.nn.sigmoid(jnp.dot(x_ref[...], w_ref[0], preferred_element_type=F32)).astype(o_ref.dtype)
    c_ref[...] = s_ref[0].astype(c_ref.dtype)


def _gate_proj_with_cast(h, w_gate, layer, w_f32):
    m, k = h.shape
    n = w_gate.shape[2]
    tn = 2 * BRANCH_W
    n_i, n_j = m // TM_MM, n // tn
    steps = n_i * n_j
    jl = layer // 2
    flat = w_f32.reshape(w_f32.shape[0], -1, w_f32.shape[-1])
    cols = flat.shape[2]
    n_slabs = 1 << (steps.bit_length() - 1)
    bf16_rows = 16
    assert flat.shape[1] % n_slabs == 0 and (flat.shape[1] // n_slabs) % bf16_rows == 0
    rows = flat.shape[1] // n_slabs

    def slab(i, j):
        return jnp.minimum(i * n_j + j, n_slabs - 1)

    vmem = 2 * (_nbytes((TM_MM, k), h.dtype) + _nbytes((k, tn), w_gate.dtype) + _nbytes((TM_MM, tn), MXU_DTYPE)
                + _nbytes((rows, cols), F32) + _nbytes((rows, cols), MXU_DTYPE)) + _nbytes((TM_MM, tn), F32)
    return pl.pallas_call(
        _gate_kernel,
        grid=(n_i, n_j),
        in_specs=[pl.BlockSpec((TM_MM, k), lambda i, j: (i, 0)),
                  pl.BlockSpec((1, k, tn), lambda i, j: (layer, 0, j)),
                  pl.BlockSpec((1, rows, cols), lambda i, j: (jl, slab(i, j), 0))],
        out_specs=[pl.BlockSpec((TM_MM, tn), lambda i, j: (i, j)),
                   pl.BlockSpec((rows, cols), lambda i, j: (slab(i, j), 0))],
        out_shape=[jax.ShapeDtypeStruct((m, n), MXU_DTYPE), jax.ShapeDtypeStruct(flat.shape[1:], MXU_DTYPE)],
        compiler_params=_params(("arbitrary", "arbitrary"), vmem),
        name="gate_proj_cast",
    )(h, w_gate, flat)


def _mm_nt_kernel(w_ref, x_ref, o_ref):
    o_ref[...] = lax.dot_general(w_ref[0], x_ref[...], (((1,), (1,)), ((), ())),
                                 preferred_element_type=F32)


def _mm_nt(w_t, x, layer):
    m, k = x.shape
    n = w_t.shape[1]
    vmem = 2 * (_nbytes((TM_MM, k), x.dtype) + _nbytes((n, k), w_t.dtype) + _nbytes((n, TM_MM), F32))
    return pl.pallas_call(
        _mm_nt_kernel,
        grid=(m // TM_MM,),
        in_specs=[pl.BlockSpec((1, n, k), lambda i: (layer, 0, 0)),
                  pl.BlockSpec((TM_MM, k), lambda i: (i, 0))],
        out_specs=pl.BlockSpec((n, TM_MM), lambda i: (0, i)),
        out_shape=jax.ShapeDtypeStruct((n, m), F32),
        compiler_params=_params(("parallel",), vmem),
        name="ssm_proj",
    )(w_t, x)


def _mm_res_kernel(a_ref, w_ref, x_ref, g_ref, o_ref):
    y = jnp.dot(a_ref[...], w_ref[0], preferred_element_type=F32)
    o_ref[...] = x_ref[...] + g_ref[0, 0] * y


def _mm_residual(a, w, layer, x, mods, k_gate, n_ctx, lat_len):
    m, k = a.shape
    n = w.shape[2]
    vmem = 2 * (_nbytes((TM, k), a.dtype) + _nbytes((k, n), w.dtype) + 2 * _nbytes((TM, n), F32))
    return pl.pallas_call(
        _mm_res_kernel,
        grid=(m // TM,),
        in_specs=[pl.BlockSpec((TM, k), lambda i: (i, 0)),
                  pl.BlockSpec((1, k, n), lambda i: (layer, 0, 0)),
                  pl.BlockSpec((TM, n), lambda i: (i, 0)),
                  _mod_spec(layer, k_gate, TM, n_ctx, lat_len)],
        out_specs=pl.BlockSpec((TM, n), lambda i: (i, 0)),
        out_shape=jax.ShapeDtypeStruct((m, n), F32),
        compiler_params=_params(("parallel",), vmem),
        name="out_proj",
    )(a, w, x, mods)


PW_LF, PW_RF, PW_LB, PW_RB, PW_SF, PW_SB, PW_OF, PW_OB = range(8)
V_LF, V_LB, V_RF, V_RB, V_SF, V_SB, V_OF, V_OB = (2 * k for k in range(8))


def _ssm_kernel(u_ref, pw_ref, vec_ref, lam_ref, h0_ref, d_ref, y_ref, fin_ref,
                lf_ref, lb_ref, rf_ref, rb_ref, wst_ref, wout_ref, toep_ref, sf_ref, sb_ref, hf_ref, hb_ref,
                *, ctx_seqs, ctx_chunks, lat_seqs, lat_chunks):
    kdim = SSM_GROUP * CHUNK
    half = 2 * SSM_STATE

    def fill(dst_ref, lanes, table, vset):
        tab = pw_ref[0, 0, table]
        tab_sw = _swap_halves(tab)
        for j in range(SSM_GROUP):
            v1 = vec_ref[0, 0, vset, j:j + 1, :]
            v2 = vec_ref[0, 0, vset + 1, j:j + 1, :]
            dst_ref[j * CHUNK:(j + 1) * CHUNK, lanes] = (tab * v1 + tab_sw * v2).astype(dst_ref.dtype)

    fwd, bwd = slice(0, half), slice(half, 2 * half)
    fill(lf_ref, fwd, PW_LF, V_LF)
    fill(lb_ref, fwd, PW_LB, V_LB)
    fill(rf_ref, fwd, PW_RF, V_RF)
    fill(rb_ref, fwd, PW_RB, V_RB)
    fill(wst_ref, fwd, PW_SF, V_SF)
    fill(wst_ref, bwd, PW_SB, V_SB)
    fill(wout_ref, fwd, PW_OF, V_OF)
    fill(wout_ref, bwd, PW_OB, V_OB)

    row_s = lax.broadcasted_iota(jnp.int32, (CHUNK, kdim), 0)
    col_t = lax.broadcasted_iota(jnp.int32, (CHUNK, kdim), 1) & (CHUNK - 1)
    causal = col_t >= row_s
    anti = col_t <= row_s
    nt_dims = (((1,), (1,)), ((), ()))

    def build(j, carry):
        rows = pl.ds(pl.multiple_of(j * CHUNK, CHUNK), CHUNK)
        a = lax.dot_general(lf_ref[rows, :], rf_ref[...], nt_dims, preferred_element_type=F32)
        b = lax.dot_general(lb_ref[rows, :], rb_ref[...], nt_dims, preferred_element_type=F32)
        toep_ref[rows, :] = (jnp.where(causal, a, 0.0) + jnp.where(anti, b, 0.0)).astype(toep_ref.dtype)
        return carry

    lax.fori_loop(0, SSM_GROUP, build, 0, unroll=4)

    u = jnp.concatenate([u_ref[0, j] for j in range(SSM_GROUP)], axis=-1)
    ub = u.astype(MXU_DTYPE)
    s_all = jnp.dot(ub, wst_ref[...], preferred_element_type=F32)
    sf_ref[...] = s_all[:, fwd]
    sb_ref[...] = s_all[:, bwd]

    def advance(h, s, lanes):
        return h * lam_ref[0, 0, 0:1, lanes] + _swap_halves(h) * lam_ref[0, 0, 1:2, lanes] + s

    def chain(base, seqs, chunks, h_fwd, h_bwd):
        rows = lambda c: pl.ds(base + c, seqs, stride=chunks) if chunks > 1 else pl.ds(base, seqs)
        for c in range(chunks):
            hf_ref[rows(c), :] = h_fwd
            h_fwd = advance(h_fwd, sf_ref[rows(c), :], fwd)
        for c in reversed(range(chunks)):
            hb_ref[rows(c), :] = h_bwd
            h_bwd = advance(h_bwd, sb_ref[rows(c), :], bwd)
        return h_fwd, h_bwd

    zeros = jnp.zeros((ctx_seqs, half), F32)
    end_fwd, end_bwd = chain(0, ctx_seqs, ctx_chunks, zeros, zeros)
    fin_ref[0, :, fwd] = end_fwd
    fin_ref[0, :, bwd] = end_bwd
    chain(ctx_seqs * ctx_chunks, lat_seqs, lat_chunks, h0_ref[0, 0, :, fwd], h0_ref[0, 0, :, bwd])

    hin = jnp.concatenate([hf_ref[...], hb_ref[...]], axis=-1).astype(MXU_DTYPE)
    nblk = 4
    for q in range(SSM_GROUP // nblk):
        cols = slice(q * nblk * CHUNK, (q + 1) * nblk * CHUNK)
        y = (jnp.dot(ub, toep_ref[:, cols], preferred_element_type=F32)
             + lax.dot_general(hin, wout_ref[cols, :], nt_dims, preferred_element_type=F32)
             + u[:, cols] * d_ref[0, 0][:, cols])
        for r in range(nblk):
            y_ref[0, q * nblk + r] = y[:, r * CHUNK:(r + 1) * CHUNK]


def _ssm(u_t, tabs, layer, ctx_seqs, ctx_chunks, lat_seqs, lat_chunks):
    g, hh, rows, _ = u_t.shape
    kdim = SSM_GROUP * CHUNK
    ns = 4 * SSM_STATE
    half = 2 * SSM_STATE
    n_pw, n_vec = tabs["pw"].shape[2], tabs["vec"].shape[2]
    vmem = (2 * (2 * _nbytes((hh, rows, CHUNK), F32) + _nbytes((n_pw, CHUNK, LANES), F32)
                 + _nbytes((n_vec, SSM_GROUP, LANES), F32))
            + 4 * _nbytes((kdim, half), MXU_DTYPE) + 2 * _nbytes((kdim, ns), MXU_DTYPE)
            + _nbytes((kdim, kdim), MXU_DTYPE) + 2 * _nbytes((rows, ns), F32)
            + 3 * _nbytes((rows, kdim), F32))
    per_lg = lambda *shape: pl.BlockSpec((1, 1) + shape, lambda i: (layer, i) + (0,) * len(shape))
    per_g = lambda *shape: pl.BlockSpec((1,) + shape, lambda i: (i,) + (0,) * len(shape))
    return pl.pallas_call(
        functools.partial(_ssm_kernel, ctx_seqs=ctx_seqs, ctx_chunks=ctx_chunks,
                          lat_seqs=lat_seqs, lat_chunks=lat_chunks),
        grid=(g,),
        in_specs=[per_g(hh, rows, CHUNK),
                  per_lg(n_pw, CHUNK, LANES), per_lg(n_vec, SSM_GROUP, LANES),
                  per_lg(2, ns), per_lg(lat_seqs, ns), per_lg(1, kdim)],
        out_specs=[per_g(hh, rows, CHUNK), per_g(ctx_seqs, ns)],
        out_shape=[jax.ShapeDtypeStruct(u_t.shape, F32),
                   jax.ShapeDtypeStruct((g, ctx_seqs, ns), F32)],
        scratch_shapes=[pltpu.VMEM((kdim, half), MXU_DTYPE), pltpu.VMEM((kdim, half), MXU_DTYPE),
                        pltpu.VMEM((kdim, half), MXU_DTYPE), pltpu.VMEM((kdim, half), MXU_DTYPE),
                        pltpu.VMEM((kdim, ns), MXU_DTYPE), pltpu.VMEM((kdim, ns), MXU_DTYPE),
                        pltpu.VMEM((kdim, kdim), MXU_DTYPE),
                        pltpu.VMEM((rows, half), F32), pltpu.VMEM((rows, half), F32),
                        pltpu.VMEM((rows, half), F32), pltpu.VMEM((rows, half), F32)],
        compiler_params=_params(("parallel",), vmem),
        name="s5_chunked",
    )(u_t, tabs["pw"], tabs["vec"], tabs["lam"], tabs["h0"], tabs["d"])


def _cmul(ar, ai, br, bi):
    return ar * br - ai * bi, ar * bi + ai * br


def _ssm_tables(lam_re, lam_im, b_re, b_im, c_re, c_im, log_dt, d_skip, state_re, state_im):
    dt = jnp.exp(log_dt)[..., None]
    ea, eb = lam_re * dt, lam_im * dt
    mag = jnp.exp(ea)
    lbar_re, lbar_im = mag * jnp.cos(eb), mag * jnp.sin(eb)
    den = lam_re * lam_re + lam_im * lam_im
    nr, ni = lbar_re - 1.0, lbar_im
    coef_re = (nr * lam_re + ni * lam_im) / den
    coef_im = (ni * lam_re - nr * lam_im) / den
    bb_re, bb_im = _cmul(coef_re[..., None], coef_im[..., None], b_re, b_im)
    bb_re, bb_im = jnp.swapaxes(bb_re, 3, 4), jnp.swapaxes(bb_im, 3, 4)

    def power(dr, n):
        n = n.astype(F32)[None, None, :, None]
        m = jnp.exp(ea[:, dr, :, None, :] * n)
        ang = eb[:, dr, :, None, :] * n
        return jnp.concatenate([m * jnp.cos(ang), m * jnp.sin(ang)], axis=-1)

    pos = jnp.arange(CHUNK)
    ctr = SSM_CENTER
    pw = jnp.stack([power(0, ctr - pos), power(0, pos - ctr), power(1, pos - ctr), power(1, ctr - pos),
                    power(0, CHUNK - 1 - pos), power(1, pos), power(0, pos + 1), power(1, CHUNK - pos)],
                   axis=2)

    cat = lambda *xs: jnp.concatenate(xs, axis=-1)
    conj_rows = lambda re, im: [cat(re, -re), cat(-im, -im)]
    plain_rows = lambda re, im: [cat(re, re), cat(-im, im)]
    vec = jnp.stack(conj_rows(bb_re[:, 0], bb_im[:, 0]) + conj_rows(bb_re[:, 1], bb_im[:, 1])
                    + plain_rows(c_re[:, 0], c_im[:, 0]) + plain_rows(c_re[:, 1], c_im[:, 1])
                    + plain_rows(bb_re[:, 0], bb_im[:, 0]) + plain_rows(bb_re[:, 1], bb_im[:, 1])
                    + conj_rows(c_re[:, 0], c_im[:, 0]) + conj_rows(c_re[:, 1], c_im[:, 1]),
                   axis=2)

    pt = power(0, jnp.array([CHUNK])), power(1, jnp.array([CHUNK]))
    p = SSM_STATE
    lam = jnp.concatenate(
        [cat(pt[0][..., :p], pt[0][..., :p], pt[1][..., :p], pt[1][..., :p]),
         cat(-pt[0][..., p:], pt[0][..., p:], -pt[1][..., p:], pt[1][..., p:])], axis=2)
    d_rows = jnp.repeat(d_skip.reshape(-1, SSM_GROUPS, 1, SSM_GROUP), CHUNK, axis=-1)
    h0 = cat(state_re[:, :, 0], state_im[:, :, 0], state_re[:, :, 1], state_im[:, :, 1]).transpose(1, 2, 0, 3)
    return {"pw": pw, "vec": vec, "lam": lam, "d": d_rows, "h0": h0}


def _glu_kernel(y_ref, w_ref, o_ref):
    z = jax.nn.gelu(y_ref[...])
    gate = jax.nn.sigmoid(jnp.dot(w_ref[0], z.astype(MXU_DTYPE), preferred_element_type=F32))
    o_ref[...] = (z * gate).T.astype(o_ref.dtype)


def _glu(y_t, w_t, layer):
    w_, nt = y_t.shape
    vmem = 2 * (_nbytes((w_, TM), F32) + _nbytes((w_, w_), MXU_DTYPE) + _nbytes((TM, w_), MXU_DTYPE))
    return pl.pallas_call(
        _glu_kernel,
        grid=(nt // TM,),
        in_specs=[pl.BlockSpec((w_, TM), lambda i: (0, i)),
                  pl.BlockSpec((1, w_, w_), lambda i: (layer, 0, 0))],
        out_specs=pl.BlockSpec((TM, w_), lambda i: (i, 0)),
        out_shape=jax.ShapeDtypeStruct((nt, w_), MXU_DTYPE),
        compiler_params=_params(("parallel",), vmem),
        name="s5_glu",
    )(y_t, w_t)


Z_SGU_BLOCK, Z_FFT_BLOCK, Z_CONV_BLOCK = 0, 2, 3


def _fft_kernel(x_ref, cs_ref, dft_ref, o_ref, pq_ref):
    seq_len = x_ref.shape[0]

    @pl.when(pl.program_id(1) == 0)
    def _():
        pq = jnp.dot(x_ref[...].astype(MXU_DTYPE), cs_ref[...], preferred_element_type=F32)
        pq_ref[:seq_len, :] = pq[:, :BRANCH_W].astype(pq_ref.dtype)
        pq_ref[seq_len:, :] = pq[:, BRANCH_W:].astype(pq_ref.dtype)

    o_ref[...] = jnp.dot(dft_ref[...], pq_ref[...], preferred_element_type=F32).astype(o_ref.dtype)


def _dft_tables(seq_len):
    def cs(n):
        k = jnp.arange(n, dtype=jnp.int32)
        ang = ((k[:, None] * k[None, :]) % n).astype(F32) * (2.0 * math.pi / n)
        return jnp.cos(ang) / math.sqrt(n), jnp.sin(ang) / math.sqrt(n)

    cl, sl = cs(seq_len)
    cc, sc = cs(CHUNK)
    eye = jnp.eye(BRANCH_W // CHUNK, dtype=F32)
    chan = jnp.concatenate([jnp.kron(eye, cc), jnp.kron(eye, sc)], axis=1)
    return jnp.concatenate([cl, -sl], axis=1).astype(MXU_DTYPE), chan.astype(MXU_DTYPE)


def _fft(z, row0, seqs, seq_len):
    dft, chan = _dft_tables(seq_len)
    tm = min(seq_len, TM)
    vmem = (2 * (_nbytes((seq_len, BRANCH_W), F32) + _nbytes(chan.shape, MXU_DTYPE)
                 + _nbytes((tm, 2 * seq_len), MXU_DTYPE) + _nbytes((tm, BRANCH_W), MXU_DTYPE))
            + _nbytes((2 * seq_len, BRANCH_W), MXU_DTYPE) + _nbytes((seq_len, 2 * BRANCH_W), F32))
    seq0 = row0 // seq_len
    return pl.pallas_call(
        _fft_kernel,
        grid=(seqs, seq_len // tm),
        in_specs=[pl.BlockSpec((seq_len, BRANCH_W), lambda b, j: (seq0 + b, Z_FFT_BLOCK)),
                  pl.BlockSpec(chan.shape, lambda b, j: (0, 0)),
                  pl.BlockSpec((tm, 2 * seq_len), lambda b, j: (j, 0))],
        out_specs=pl.BlockSpec((tm, BRANCH_W), lambda b, j: (b * (seq_len // tm) + j, 0)),
        out_shape=jax.ShapeDtypeStruct((seqs * seq_len, BRANCH_W), MXU_DTYPE),
        scratch_shapes=[pltpu.VMEM((2 * seq_len, BRANCH_W), MXU_DTYPE)],
        compiler_params=_params(("parallel", "arbitrary"), vmem),
        name="fnet_dft",
    )(z, chan, dft)


def _sgu_kernel(z_ref, g_ref, ws_ref, bs_ref, o_ref):
    uv = jax.nn.gelu(z_ref[...])
    u = uv[:, :BRANCH_W]
    v = _rmsnorm(uv[:, BRANCH_W:], g_ref[0]).astype(MXU_DTYPE)
    head_w = BRANCH_W // SGU_HEADS
    for c in range(z_ref.shape[0] // CHUNK):
        rows = slice(c * CHUNK, (c + 1) * CHUNK)
        s = jnp.concatenate(
            [jnp.dot(ws_ref[0, h], v[rows, h * head_w:(h + 1) * head_w], preferred_element_type=F32)
             for h in range(SGU_HEADS)], axis=-1)
        o_ref[rows, :] = (u[rows, :] * (s + bs_ref[0])).astype(o_ref.dtype)


def _sgu(z, g, w_s, bias, layer):
    nt = z.shape[0]
    vmem = 2 * (_nbytes((TM, 2 * BRANCH_W), F32) + _nbytes((TM, BRANCH_W), MXU_DTYPE)) + 3 * _nbytes((TM, 2 * BRANCH_W), F32)
    return pl.pallas_call(
        _sgu_kernel,
        grid=(nt // TM,),
        in_specs=[pl.BlockSpec((TM, 2 * BRANCH_W), lambda i: (i, Z_SGU_BLOCK)),
                  pl.BlockSpec((1, 1, BRANCH_W), lambda i: (layer, 0, 0)),
                  pl.BlockSpec((1, SGU_HEADS, CHUNK, CHUNK), lambda i: (layer, 0, 0, 0)),
                  pl.BlockSpec((1, CHUNK, BRANCH_W), lambda i: (layer, 0, 0))],
        out_specs=pl.BlockSpec((TM, BRANCH_W), lambda i: (i, 0)),
        out_shape=jax.ShapeDtypeStruct((nt, BRANCH_W), MXU_DTYPE),
        compiler_params=_params(("parallel",), vmem),
        name="sgu",
    )(z, g, w_s, bias)


def _conv_kernel(x_ref, b_ref, c_ref, w_ref, o_ref, *, n_ctx_tiles, ctx_period, lat_period):
    t = c_ref[...] * x_ref[...]
    tm = t.shape[0]
    period = jnp.where(pl.program_id(0) < n_ctx_tiles, ctx_period, lat_period)
    pos = lax.broadcasted_iota(jnp.int32, t.shape, 0) & (period - 1)
    prev = jnp.where(pos == 0, 0.0, pltpu.roll(t, 1, axis=0))
    nxt = jnp.where(pos == period - 1, 0.0, pltpu.roll(t, tm - 1, axis=0))
    w = w_ref[0]
    o_ref[...] = (b_ref[...] * (prev * w[0:1] + t * w[1:2] + nxt * w[2:3])).astype(o_ref.dtype)


def _conv(z, w_taps, layer, n_ctx, ctx_period):
    nt = z.shape[0]
    col = lambda k: pl.BlockSpec((TM, BRANCH_W), lambda i: (i, Z_CONV_BLOCK + k))
    vmem = 2 * (3 * _nbytes((TM, BRANCH_W), F32) + _nbytes((TM, BRANCH_W), MXU_DTYPE)) + 4 * _nbytes((TM, BRANCH_W), F32)
    return pl.pallas_call(
        functools.partial(_conv_kernel, n_ctx_tiles=n_ctx // TM, ctx_period=ctx_period, lat_period=GRID_W),
        grid=(nt // TM,),
        in_specs=[col(0), col(1), col(2),
                  pl.BlockSpec((1,) + w_taps.shape[1:], lambda i: (layer, 0, 0))],
        out_specs=pl.BlockSpec((TM, BRANCH_W), lambda i: (i, 0)),
        out_shape=jax.ShapeDtypeStruct((nt, BRANCH_W), MXU_DTYPE),
        compiler_params=_params(("parallel",), vmem),
        name="short_conv",
    )(z, z, z, w_taps)


def _merge_kernel(b0_ref, b1c_ref, b1l_ref, b2_ref, b3_ref, gate_ref, wup_ref, o_ref, *, n_ctx_tiles):
    fft = jnp.where(pl.program_id(0) < n_ctx_tiles, b1c_ref[...], b1l_ref[...])
    acc = None
    for n, b in enumerate((b0_ref[...], fft, b2_ref[...], b3_ref[...])):
        up = jnp.dot(b, wup_ref[0, n], preferred_element_type=F32)
        term = gate_ref[:, n * D_MODEL:(n + 1) * D_MODEL].astype(F32) * up
        acc = term if acc is None else acc + term
    o_ref[...] = acc.astype(o_ref.dtype)


def _merge(branches, gates, w_up, layer, n_ctx):
    nt = gates.shape[0]
    n_ctx_tiles = n_ctx // TM
    vmem = (2 * (5 * _nbytes((TM, BRANCH_W), MXU_DTYPE) + _nbytes((TM, N_BRANCH * D_MODEL), MXU_DTYPE)
                 + _nbytes(w_up.shape[1:], MXU_DTYPE) + _nbytes((TM, D_MODEL), MXU_DTYPE))
            + 2 * _nbytes((TM, D_MODEL), F32))
    br = pl.BlockSpec((TM, BRANCH_W), lambda i: (i, 0))
    br_ctx = pl.BlockSpec((TM, BRANCH_W), lambda i: (jnp.minimum(i, n_ctx_tiles - 1), 0))
    br_lat = pl.BlockSpec((TM, BRANCH_W), lambda i: (jnp.maximum(i - n_ctx_tiles, 0), 0))
    return pl.pallas_call(
        functools.partial(_merge_kernel, n_ctx_tiles=n_ctx_tiles),
        grid=(nt // TM,),
        in_specs=[br, br_ctx, br_lat, br, br,
                  pl.BlockSpec((TM, N_BRANCH * D_MODEL), lambda i: (i, 0)),
                  pl.BlockSpec((1,) + w_up.shape[1:], lambda i: (layer, 0, 0, 0))],
        out_specs=pl.BlockSpec((TM, D_MODEL), lambda i: (i, 0)),
        out_shape=jax.ShapeDtypeStruct((nt, D_MODEL), MXU_DTYPE),
        compiler_params=_params(("parallel",), vmem),
        name="branch_merge",
    )(*branches, gates, w_up)


def _swiglu_partial(h, w1, w3, w2):
    a = jnp.dot(h, w1, preferred_element_type=F32)
    b = jnp.dot(h, w3, preferred_element_type=F32)
    act = (a * jax.nn.sigmoid(a) * b).astype(MXU_DTYPE)
    return jnp.dot(act, w2, preferred_element_type=F32)


def _ffn_kernel(x_ref, ng_ref, sc_ref, sh_ref, w1_ref, w3_ref, w2_ref, g_ref, o_ref, h_ref, acc_ref):
    f = pl.program_id(1)

    @pl.when(f == 0)
    def _():
        for r0 in range(0, x_ref.shape[0], TM // 2):
            rows = slice(r0, r0 + TM // 2)
            y = _rmsnorm(x_ref[rows, :], ng_ref[0]) * (1.0 + sc_ref[0, 0]) + sh_ref[0, 0]
            h_ref[rows, :] = y.astype(h_ref.dtype)
        acc_ref[...] = jnp.zeros_like(acc_ref)

    acc_ref[...] += _swiglu_partial(h_ref[...], w1_ref[0], w3_ref[0], w2_ref[0])

    @pl.when(f == pl.num_programs(1) - 1)
    def _():
        o_ref[...] = x_ref[...] + g_ref[0, 0] * acc_ref[...]


def _ffn(x, norm_g, w1, w3, w2, j, mods, layer, n_ctx, lat_len):
    nt, d = x.shape
    ff = w1.shape[2]
    tm = TM_MM
    once = pl.Buffered(1)
    vmem = (2 * 3 * _nbytes((d, TF), MXU_DTYPE) + 3 * _nbytes((tm, d), F32) + _nbytes((tm, d), MXU_DTYPE)
            + 3 * _nbytes((tm, TF), F32) + 4 * _nbytes((TM // 2, d), F32))
    return pl.pallas_call(
        _ffn_kernel,
        grid=(nt // tm, ff // TF),
        in_specs=[pl.BlockSpec((tm, d), lambda i, f: (i, 0), pipeline_mode=once),
                  pl.BlockSpec((1, 1, d), lambda i, f: (layer, 0, 0)),
                  _mod_spec(layer, 4, tm, n_ctx, lat_len),
                  _mod_spec(layer, 3, tm, n_ctx, lat_len),
                  pl.BlockSpec((1, d, TF), lambda i, f: (j, 0, f)),
                  pl.BlockSpec((1, d, TF), lambda i, f: (j, 0, f)),
                  pl.BlockSpec((1, TF, d), lambda i, f: (j, f, 0)),
                  _mod_spec(layer, 5, tm, n_ctx, lat_len)],
        out_specs=pl.BlockSpec((tm, d), lambda i, f: (i, 0), pipeline_mode=once),
        out_shape=jax.ShapeDtypeStruct((nt, d), F32),
        scratch_shapes=[pltpu.VMEM((tm, d), MXU_DTYPE), pltpu.VMEM((tm, d), F32)],
        compiler_params=_params(("parallel", "arbitrary"), vmem),
        name="ffn_dense",
    )(x, norm_g, mods, mods, w1, w3, w2, mods)


def _dispatch_kernel(dest_ref, hp_ref, init_ref, xs_ref, sem):
    del init_ref
    n = hp_ref.shape[0]

    def row_copy(r, dst_row):
        return pltpu.make_async_copy(hp_ref.at[r], xs_ref.at[dst_row], sem)

    def start(r, carry):
        row_copy(r, dest_ref[0, r]).start()
        row_copy(r, dest_ref[1, r]).start()
        return carry

    def wait(r, carry):
        row_copy(r, 0).wait()
        row_copy(r, 0).wait()
        return carry

    lax.fori_loop(0, n, start, 0, unroll=8)
    lax.fori_loop(0, n, wait, 0, unroll=8)


def _dispatch(hp, dest, n_rows):
    nt = hp.shape[0]
    tile = hp.shape[1:]
    vmem = 2 * _nbytes((MOVE_ROWS,) + tile, hp.dtype)
    return pl.pallas_call(
        _dispatch_kernel,
        grid=(nt // MOVE_ROWS,),
        in_specs=[pl.BlockSpec((2, MOVE_ROWS), lambda i: (0, i), memory_space=pltpu.SMEM),
                  pl.BlockSpec((MOVE_ROWS,) + tile, lambda i: (i, 0, 0)),
                  pl.BlockSpec(memory_space=pl.ANY)],
        out_specs=pl.BlockSpec(memory_space=pl.ANY),
        out_shape=jax.ShapeDtypeStruct((n_rows,) + tile, hp.dtype),
        scratch_shapes=[pltpu.SemaphoreType.DMA(())],
        input_output_aliases={2: 0},
        compiler_params=_params(("arbitrary",), vmem),
        name="moe_dispatch",
    )(dest, hp, jnp.zeros((n_rows,) + tile, hp.dtype))


def _moe_ffn_kernel(te_ref, nv_ref, hp_ref, w1_ref, w3_ref, w2_ref, o_ref, h_ref, acc_ref):
    i = pl.program_id(0)
    f = pl.program_id(1)
    last = pl.num_programs(1) - 1

    @pl.when(i < nv_ref[0])
    def _():
        @pl.when(f == 0)
        def _():
            half = h_ref.shape[1] // 2
            lo, hi = _unpack_bf16_pair(_tiles_to_blocks(hp_ref[...]))
            for q in range(ROW_TILES):
                h_ref[:, q * LANES:(q + 1) * LANES] = lo[q].astype(h_ref.dtype)
                h_ref[:, half + q * LANES:half + (q + 1) * LANES] = hi[q].astype(h_ref.dtype)
            acc_ref[...] = jnp.zeros_like(acc_ref)

        acc_ref[...] += _swiglu_partial(h_ref[...], w1_ref[0], w3_ref[0], w2_ref[0, 0])

        @pl.when(f == last)
        def _():
            o_ref[...] = _rows_to_tiles(acc_ref[...])

    @pl.when(jnp.logical_and(i >= nv_ref[0], f == last))
    def _():
        o_ref[...] = jnp.zeros_like(o_ref)


def _moe_ffn(xs, w1, w3, w2, j, tile_expert, n_valid):
    r = xs.shape[0]
    d = D_MODEL
    ff = w1.shape[2]
    tm, tf = MOE_TM, MOE_TF
    vmem = (2 * (_nbytes((tm, d // 2), U32) + 3 * _nbytes((d, tf), MXU_DTYPE) + _nbytes((tm, d), F32))
            + _nbytes((tm, d), MXU_DTYPE) + _nbytes((tm, d), F32) + 3 * _nbytes((tm, tf), F32))

    def live(i, f, nv):
        return jnp.where(i < nv[0], f, 0)

    grid_spec = pltpu.PrefetchScalarGridSpec(
        num_scalar_prefetch=2,
        grid=(r // tm, ff // tf),
        in_specs=[pl.BlockSpec((tm, ROW_TILES, LANES), lambda i, f, te, nv: (i, 0, 0)),
                  pl.BlockSpec((1, d, tf), lambda i, f, te, nv: (te[i], 0, live(i, f, nv))),
                  pl.BlockSpec((1, d, tf), lambda i, f, te, nv: (te[i], 0, live(i, f, nv))),
                  pl.BlockSpec((1, 1, tf, d), lambda i, f, te, nv: (j, te[i], live(i, f, nv), 0))],
        out_specs=pl.BlockSpec((tm, OUT_TILES, LANES), lambda i, f, te, nv: (i, 0, 0)),
        scratch_shapes=[pltpu.VMEM((tm, d), MXU_DTYPE), pltpu.VMEM((tm, d), F32)],
    )
    return pl.pallas_call(
        _moe_ffn_kernel,
        grid_spec=grid_spec,
        out_shape=jax.ShapeDtypeStruct((r, OUT_TILES, LANES), F32),
        compiler_params=_params(("arbitrary", "arbitrary"), vmem),
        name="moe_ffn",
    )(tile_expert, n_valid, xs, w1, w3, w2)


def _combine_kernel(dest_ref, next_ref, ys_ref, p_ref, x_ref, g_ref, o_ref, buf_ref, sems):
    step = pl.program_id(0)
    n = o_ref.shape[0]
    slot = lax.rem(step, 2)

    def row_copy(s, k, r, src_row):
        return pltpu.make_async_copy(ys_ref.at[src_row], buf_ref.at[s, k, r], sems.at[s])

    def request(idx_ref, s):
        def start(r, carry):
            row_copy(s, 0, r, idx_ref[0, r]).start()
            row_copy(s, 1, r, idx_ref[1, r]).start()
            return carry
        lax.fori_loop(0, n, start, 0, unroll=8)

    @pl.when(step == 0)
    def _():
        request(dest_ref, 0)

    @pl.when(step + 1 < pl.num_programs(0))
    def _():
        request(next_ref, 1 - slot)

    def wait(r, carry):
        row_copy(slot, 0, r, 0).wait()
        row_copy(slot, 1, r, 0).wait()
        return carry

    lax.fori_loop(0, n, wait, 0, unroll=8)
    p1 = jnp.broadcast_to(p_ref[:, 0:1], (n, LANES))
    p2 = jnp.broadcast_to(p_ref[:, 1:2], (n, LANES))
    first, second = _tiles_to_blocks(buf_ref[slot, 0]), _tiles_to_blocks(buf_ref[slot, 1])
    for q in range(OUT_TILES):
        cols = slice(q * LANES, (q + 1) * LANES)
        o_ref[:, cols] = x_ref[:, cols] + g_ref[0, 0][:, cols] * (p1 * first[q] + p2 * second[q])


def _combine(ys, dest, probs, x, mods, layer, k_gate, n_ctx, lat_len):
    nt, d = x.shape
    tc = MOVE_ROWS
    n_steps = nt // tc
    vmem = 2 * (2 * _nbytes((tc, d), F32) + _nbytes((tc, LANES), F32)) + 3 * _nbytes((2, tc, d), F32)
    return pl.pallas_call(
        _combine_kernel,
        grid=(n_steps,),
        in_specs=[pl.BlockSpec((2, tc), lambda i: (0, i), memory_space=pltpu.SMEM),
                  pl.BlockSpec((2, tc), lambda i: (0, jnp.minimum(i + 1, n_steps - 1)), memory_space=pltpu.SMEM),
                  pl.BlockSpec(memory_space=pl.ANY),
                  pl.BlockSpec((tc, LANES), lambda i: (i, 0)),
                  pl.BlockSpec((tc, d), lambda i: (i, 0)),
                  _mod_spec(layer, k_gate, tc, n_ctx, lat_len)],
        out_specs=pl.BlockSpec((tc, d), lambda i: (i, 0)),
        out_shape=jax.ShapeDtypeStruct((nt, d), F32),
        scratch_shapes=[pltpu.VMEM((2, 2, tc, OUT_TILES, LANES), F32), pltpu.SemaphoreType.DMA((2,))],
        compiler_params=_params(("arbitrary",), vmem),
        name="moe_combine",
    )(dest, dest, ys, probs, x, mods)


def _route(idx, n_rows):
    nt = idx.shape[1]
    e_flat = idx.reshape(-1)
    onehot = (e_flat[:, None] == jnp.arange(N_EXPERTS)[None, :]).astype(jnp.int32)
    counts = jnp.sum(onehot, axis=0)
    rank = jnp.sum((jnp.cumsum(onehot, axis=0) - onehot) * onehot, axis=1)
    padded = ((counts + MOE_TM - 1) // MOE_TM) * MOE_TM
    ends = jnp.cumsum(padded)
    starts = ends - padded
    dest = jnp.sum(onehot * starts[None, :], axis=1) + rank
    tile_start = jnp.arange(n_rows // MOE_TM, dtype=jnp.int32) * MOE_TM
    tile_expert = jnp.minimum(jnp.sum((tile_start[:, None] >= ends[None, :]).astype(jnp.int32), axis=1),
                              N_EXPERTS - 1).astype(jnp.int32)
    n_valid = (ends[-1] // MOE_TM).astype(jnp.int32).reshape(1)
    return dest.reshape(2, nt).astype(jnp.int32), tile_expert, n_valid


def _moe(x, norm_g, mods, layer, w_r_t, b_r, w1, w3, w2, j, n_ctx, lat_len):
    nt = x.shape[0]
    hp, idx, probs = _norm_router(x, norm_g, mods, layer, 4, 3, w_r_t, b_r, j, n_ctx, lat_len)
    n_rows = 2 * nt + N_EXPERTS * MOE_TM
    dest, tile_expert, n_valid = _route(idx, n_rows)
    xs = _dispatch(hp, dest, n_rows)
    d = x.shape[1]
    ys = _moe_ffn(xs, w1.reshape(N_EXPERTS, d, -1), w3.reshape(N_EXPERTS, d, -1), w2, j, tile_expert, n_valid)
    return _combine(ys, dest, probs, x, mods, layer, 5, n_ctx, lat_len)


def _mixer(x, mods, layer, wts, dims, moe_w_f32):
    n_ctx, ctx_seqs, ctx_len, lat_seqs, lat_len = dims
    nt = x.shape[0]
    h = _norm_mod(x, wts["norm1"], mods, layer, 1, 0, n_ctx, lat_len)
    z = _mm(h, wts["w_in_rest"], layer, F32, 3 * BRANCH_W, name="in_proj")
    gates, moe_w_cast = _gate_proj_with_cast(h, wts["w_gate"], layer, moe_w_f32)

    u_t = _mm_nt(wts["w_in_ssm_t"], h, layer).reshape(SSM_GROUPS, SSM_GROUP, nt // CHUNK, CHUNK)
    y_t, fin = _ssm(u_t, wts["ssm"], layer, ctx_seqs, ctx_len // CHUNK, lat_seqs, lat_len // CHUNK)
    br_ssm = _glu(y_t.reshape(BRANCH_W, nt), wts["w_glu_t"], layer)

    br_fft_ctx, br_fft_lat = _fft(z, 0, ctx_seqs, ctx_len), _fft(z, n_ctx, lat_seqs, lat_len)
    br_sgu = _sgu(z, wts["sgu_norm"], wts["w_s"], wts["sgu_bias"], layer)
    br_conv = _conv(z, wts["conv_taps"], layer, n_ctx, ctx_len)

    merged = _merge((br_ssm, br_fft_ctx, br_fft_lat, br_sgu, br_conv), gates, wts["w_up"], layer, n_ctx)
    x = _mm_residual(merged, wts["w_o"], layer, x, mods, 2, n_ctx, lat_len)
    return x, fin, moe_w_cast


def kernel(x_prompt, x_sample, c, state_ssm_re, state_ssm_im, c_ctx, norm1_g, norm2_g, w_ada, b_ada, w_in, w_gate,
           ssm_lam_re, ssm_lam_im, ssm_b_re, ssm_b_im, ssm_c_re, ssm_c_im, ssm_log_dt, ssm_d, ssm_w_glu,
           sgu_norm_g, sgu_w_spatial, sgu_b_spatial, conv_w, w_up, w_o, ffn_w1, ffn_w3, ffn_w2,
           moe_w_router, moe_b_router, moe_w1, moe_w3, moe_w2, final_norm_g):
    ctx_seqs, ctx_len, d = x_prompt.shape
    lat_seqs, lat_len, _ = x_sample.shape
    depth = w_in.shape[0]
    n_ctx = ctx_seqs * ctx_len
    dims = (n_ctx, ctx_seqs, ctx_len, lat_seqs, lat_len)
    assert n_ctx % lat_len == 0 and lat_len % TM_MM == 0 and n_ctx % TM_MM == 0 and 1 + lat_seqs <= COND_ROWS
    assert ctx_len % CHUNK == 0 and ctx_len & (ctx_len - 1) == 0 and lat_len % GRID_W == 0 and TM % ctx_len == 0

    x = jnp.concatenate([x_prompt.reshape(n_ctx, d), x_sample.reshape(lat_seqs * lat_len, d)], axis=0)
    cond = jnp.concatenate([c_ctx[None, :], c, jnp.zeros((COND_ROWS - 1 - lat_seqs, d), F32)], axis=0)
    mods = _ada(cond, w_ada, b_ada).reshape(depth, COND_ROWS * N_MOD, 1, d)

    cast = lambda w: w.astype(MXU_DTYPE)
    b1, b2, b4 = BRANCH_W, 2 * BRANCH_W, 4 * BRANCH_W
    head_w = BRANCH_W // SGU_HEADS
    wts = {
        "norm1": norm1_g.reshape(depth, 1, d),
        "w_in_rest": cast(jnp.concatenate([w_in[:, :, b2:b4], w_in[:, :, b1:b2], w_in[:, :, b4:]], axis=2)),
        "w_in_ssm_t": cast(jnp.swapaxes(w_in[:, :, :b1], 1, 2)),
        "w_gate": cast(w_gate),
        "ssm": _ssm_tables(ssm_lam_re, ssm_lam_im, ssm_b_re, ssm_b_im, ssm_c_re, ssm_c_im, ssm_log_dt, ssm_d,
                           state_ssm_re, state_ssm_im),
        "w_glu_t": cast(jnp.swapaxes(ssm_w_glu, 1, 2)),
        "sgu_norm": sgu_norm_g.reshape(depth, 1, BRANCH_W),
        "w_s": cast(sgu_w_spatial),
        "sgu_bias": jnp.repeat(jnp.swapaxes(sgu_b_spatial, 1, 2), head_w, axis=2),
        "conv_taps": jnp.pad(jnp.swapaxes(conv_w, 1, 2), ((0, 0), (0, 5), (0, 0))),
        "w_up": cast(w_up), "w_o": cast(w_o),
    }
    norm2 = norm2_g.reshape(depth, 1, d)
    pad_ff = D_FF_PAD - D_FF
    ffn_w = (cast(jnp.pad(ffn_w1, ((0, 0), (0, 0), (0, pad_ff)))),
             cast(jnp.pad(ffn_w3, ((0, 0), (0, 0), (0, pad_ff)))),
             cast(jnp.pad(ffn_w2, ((0, 0), (0, pad_ff), (0, 0)))))
    moe_w2_cast = cast(moe_w2)
    w_r_t = jnp.swapaxes(moe_w_router, 1, 2)
    b_r = moe_b_router[:, :, None]

    new_re, new_im = [], []
    p = SSM_STATE
    for i in range(depth):
        x, fin, w_cast = _mixer(x, mods, i, wts, dims, moe_w1 if i % 2 == 0 else moe_w3)
        moe_w13 = (w_cast,) if i % 2 == 0 else moe_w13 + (w_cast,)
        fin = fin.transpose(1, 0, 2)
        new_re.append(jnp.stack([fin[..., :p], fin[..., 2 * p:3 * p]], axis=1))
        new_im.append(jnp.stack([fin[..., p:2 * p], fin[..., 3 * p:]], axis=1))
        j = i // 2
        if i % 2 == 0:
            x = _ffn(x, norm2, *ffn_w, j, mods, i, n_ctx, lat_len)
        else:
            x = _moe(x, norm2, mods, i, w_r_t, b_r, *moe_w13, moe_w2_cast, j, n_ctx, lat_len)

    y_prompt = _final_norm(x, final_norm_g, 0, n_ctx).reshape(ctx_seqs, ctx_len, d)
    y_sample = _final_norm(x, final_norm_g, n_ctx, lat_seqs * lat_len).reshape(lat_seqs, lat_len, d)
    return y_prompt, y_sample, jnp.stack(new_re, axis=1), jnp.stack(new_im, axis=1)
```
